```python
import math
import jax, jax.numpy as jnp
from jax import lax
import numpy as np

D_MODEL = 1024
BATCH = 4
SEQ = 8192
DEPTH = 2

N_A_LAYERS = DEPTH // 2
N_B_LAYERS = DEPTH - N_A_LAYERS
HG_HEADS = 8
HG_DIM = D_MODEL // HG_HEADS
HG_CHUNK = 32
SW_Q_HEADS = 16
SW_KV_HEADS = 4
SW_HEAD_DIM = D_MODEL // SW_Q_HEADS
SW_GROUP = SW_Q_HEADS // SW_KV_HEADS
SW_WINDOW = 128
REL_BUCKETS = 32
REL_MAX_DIST = 128
FFN_DIM = 2816
CONV_WIDTH = 3
ALPHA = (2.0 * DEPTH) ** 0.25
BETA = (8.0 * DEPTH) ** -0.25
LN_EPS = 1e-5
RMS_EPS = 1e-6

kernel_name = "yoco_hgrn2_swa_sink_convffn"


def layer_norm(x, g, b):
    xf = x.astype(jnp.float32)
    mu = xf.mean(-1, keepdims=True)
    var = jnp.square(xf - mu).mean(-1, keepdims=True)
    y = (xf - mu) * lax.rsqrt(var + LN_EPS) * g.astype(jnp.float32) + b.astype(jnp.float32)
    return y.astype(x.dtype)


def hgrn2_chunkwise(q, k, v, log_f):
    B, S, H, Dk = q.shape
    Dv = v.shape[-1]
    n = S // HG_CHUNK

    def chunks(a):
        return a.reshape(B, n, HG_CHUNK, H, a.shape[-1]).transpose(1, 0, 3, 2, 4)

    qc, kc, vc = chunks(q), chunks(k), chunks(v)
    bc = jnp.cumsum(chunks(log_f), axis=3)
    causal = jnp.tril(jnp.ones((HG_CHUNK, HG_CHUNK), dtype=bool))[:, :, None]

    def step(state, inp):
        q_, k_, v_, b_ = inp
        o_inter = jnp.einsum('bhtk,bhkv->bhtv', q_ * jnp.exp(b_), state)
        diff = b_[:, :, :, None, :] - b_[:, :, None, :, :]
        decay = jnp.where(causal, jnp.exp(jnp.minimum(diff, 0.0)), 0.0)
        scores = jnp.einsum('bhtsk,bhsk->bhts', q_[:, :, :, None, :] * decay, k_)
        o = o_inter + jnp.einsum('bhts,bhsv->bhtv', scores, v_)
        b_last = b_[:, :, -1, :]
        k_dec = k_ * jnp.exp(b_last[:, :, None, :] - b_)
        state = jnp.exp(b_last)[..., None] * state + jnp.einsum('bhsk,bhsv->bhkv', k_dec, v_)
        return state, o

    state0 = jnp.zeros((B, H, Dk, Dv), jnp.float32)
    _, o = lax.scan(step, state0, (qc, kc, vc, bc))
    return o.transpose(1, 0, 3, 2, 4).reshape(B, S, H, Dv)


def hgrn2_mixer(h, w_in, lower_bound, g_norm_w, w_out):
    B, S, D = h.shape
    q, f, i, g = jnp.split(h @ w_in, 4, axis=-1)

    def heads(a):
        return a.reshape(B, S, HG_HEADS, HG_DIM).astype(jnp.float32)

    lb = lower_bound.astype(jnp.float32).reshape(HG_HEADS, HG_DIM)
    fg = lb + (1.0 - lb) * jax.nn.sigmoid(heads(f))
    o = hgrn2_chunkwise(jax.nn.silu(heads(q)), 1.0 - fg, heads(i), jnp.log(fg))
    o = o * lax.rsqrt(jnp.mean(jnp.square(o), -1, keepdims=True) + RMS_EPS)
    o = o * g_norm_w.astype(jnp.float32) * jax.nn.silu(heads(g))
    return o.reshape(B, S, D).astype(h.dtype) @ w_out


def t5_causal_bucket(dist):
    exact = REL_BUCKETS // 2
    d = jnp.maximum(dist, 1).astype(jnp.float32)
    log_b = exact + (jnp.log(d / exact) / math.log(REL_MAX_DIST / exact)
                     * (REL_BUCKETS - exact)).astype(jnp.int32)
    return jnp.where(dist < exact, dist, jnp.minimum(log_b, REL_BUCKETS - 1))


def banded_bias_and_mask(rel_table, n_blocks):
    t = jnp.arange(SW_WINDOW)[:, None] + SW_WINDOW
    s = jnp.arange(2 * SW_WINDOW)[None, :]
    dist = t - s
    bias = rel_table[t5_causal_bucket(jnp.maximum(dist, 0))].transpose(2, 0, 1)
    band = (dist >= 0) & (dist < SW_WINDOW)
    has_prev = (jnp.arange(n_blocks) > 0)[:, None, None]
    mask = band[None] & (has_prev | (s >= SW_WINDOW)[None])
    return bias, mask


def swa_sink_mixer(h, k, v, w_q, sinks, bias, mask, w_out):
    B, S, D = h.shape
    nb = S // SW_WINDOW
    q = (h @ w_q).reshape(B, nb, SW_WINDOW, SW_KV_HEADS, SW_GROUP, SW_HEAD_DIM)

    def with_prev(a):
        ab = a.reshape(B, nb, SW_WINDOW, SW_KV_HEADS, SW_HEAD_DIM)
        prev = jnp.pad(ab, ((0, 0), (1, 0), (0, 0), (0, 0), (0, 0)))[:, :-1]
        return jnp.concatenate([prev, ab], axis=2)

    kk, vv = with_prev(k), with_prev(v)
    scale = SW_HEAD_DIM ** -0.5
    logits = jnp.einsum('bntgrd,bnsgd->bngrts', q, kk).astype(jnp.float32) * scale
    logits = logits + bias.astype(jnp.float32).reshape(SW_KV_HEADS, SW_GROUP, SW_WINDOW, 2 * SW_WINDOW)
    logits = jnp.where(mask[None, :, None, None], logits, -jnp.inf)
    sink = sinks.astype(jnp.float32).reshape(1, 1, SW_KV_HEADS, SW_GROUP, 1)
    m = jnp.maximum(logits.max(-1), sink)
    p = jnp.exp(logits - m[..., None])
    denom = p.sum(-1) + jnp.exp(sink - m)
    o = jnp.einsum('bngrts,bnsgd->bntgrd', p, vv.astype(jnp.float32))
    o = o / jnp.moveaxis(denom, -1, 2)[..., None]
    return o.reshape(B, S, D).astype(h.dtype) @ w_out


def conv_ffn(h, w_in, conv_w, conv_b, w_out):
    u = h @ w_in
    C = u.shape[-1]
    u = lax.conv_general_dilated(u, conv_w[:, None, :], window_strides=(1,),
                                 padding=[(CONV_WIDTH - 1, 0)],
                                 dimension_numbers=('NWC', 'WIO', 'NWC'),
                                 feature_group_count=C) + conv_b
    a, b = jnp.split(u, 2, axis=-1)
    return (jax.nn.silu(a) * b) @ w_out


def setup_inputs(seed: int = 0) -> dict:
    key = jax.random.key(seed)
    ks = jax.random.split(key, 24)
    D, F = D_MODEL, FFN_DIM
    kv_dim = SW_KV_HEADS * SW_HEAD_DIM
    nrm = jax.random.normal

    x = nrm(ks[0], (BATCH, SEQ, D), jnp.float32)

    hgrn_w_in = nrm(ks[1], (N_A_LAYERS, D, 4 * D), jnp.float32) * D ** -0.5
    hgrn_w_in = hgrn_w_in.at[..., 2 * D:3 * D].multiply(BETA)
    hgrn_lb_logits = nrm(ks[2], (N_A_LAYERS + 1, D), jnp.float32) * 0.5
    hgrn_gnorm_w = 1.0 + 0.02 * nrm(ks[3], (N_A_LAYERS, HG_DIM), jnp.float32)
    hgrn_w_out = nrm(ks[4], (N_A_LAYERS, D, D), jnp.float32) * D ** -0.5 * BETA

    swa_w_q = nrm(ks[5], (N_B_LAYERS, D, D), jnp.float32) * D ** -0.5
    swa_sinks = nrm(ks[6], (N_B_LAYERS, SW_Q_HEADS), jnp.float32) * 0.5
    swa_w_out = nrm(ks[7], (N_B_LAYERS, D, D), jnp.float32) * D ** -0.5 * BETA
    shared_w_kv = nrm(ks[8], (D, 2 * kv_dim), jnp.float32) * D ** -0.5
    shared_w_kv = shared_w_kv.at[:, kv_dim:].multiply(BETA)
    rel_bias = nrm(ks[9], (REL_BUCKETS, SW_Q_HEADS), jnp.float32) * 0.5

    ffn_w_in = nrm(ks[10], (DEPTH, D, 2 * F), jnp.float32) * D ** -0.5 * BETA
    ffn_conv_w = nrm(ks[11], (DEPTH, CONV_WIDTH, 2 * F), jnp.float32) * CONV_WIDTH ** -0.5
    ffn_conv_b = nrm(ks[12], (DEPTH, 2 * F), jnp.float32) * 0.02
    ffn_w_out = nrm(ks[13], (DEPTH, F, D), jnp.float32) * F ** -0.5 * BETA

    ln_mix_g = 1.0 + 0.02 * nrm(ks[14], (DEPTH, D), jnp.float32)
    ln_mix_b = 0.02 * nrm(ks[15], (DEPTH, D), jnp.float32)
    ln_ffn_g = 1.0 + 0.02 * nrm(ks[16], (DEPTH, D), jnp.float32)
    ln_ffn_b = 0.02 * nrm(ks[17], (DEPTH, D), jnp.float32)

    return {"x": x, "hgrn_w_in": hgrn_w_in, "hgrn_lb_logits": hgrn_lb_logits,
            "hgrn_gnorm_w": hgrn_gnorm_w, "hgrn_w_out": hgrn_w_out,
            "swa_w_q": swa_w_q, "swa_sinks": swa_sinks, "swa_w_out": swa_w_out,
            "shared_w_kv": shared_w_kv, "rel_bias": rel_bias,
            "ffn_w_in": ffn_w_in, "ffn_conv_w": ffn_conv_w, "ffn_conv_b": ffn_conv_b,
            "ffn_w_out": ffn_w_out, "ln_mix_g": ln_mix_g, "ln_mix_b": ln_mix_b,
            "ln_ffn_g": ln_ffn_g, "ln_ffn_b": ln_ffn_b}


def reference(x, hgrn_w_in, hgrn_lb_logits, hgrn_gnorm_w, hgrn_w_out,
              swa_w_q, swa_sinks, swa_w_out, shared_w_kv, rel_bias,
              ffn_w_in, ffn_conv_w, ffn_conv_b, ffn_w_out,
              ln_mix_g, ln_mix_b, ln_ffn_g, ln_ffn_b):
    B, S, D = x.shape
    n_blocks = S // SW_WINDOW
    lower_bounds = jnp.cumsum(jax.nn.softmax(hgrn_lb_logits.astype(jnp.float32), axis=0), axis=0)
    bias, mask = banded_bias_and_mask(rel_bias, n_blocks)

    h = x
    k_shared = v_shared = None
    for layer in range(DEPTH):
        if layer < N_A_LAYERS:
            mix = hgrn2_mixer(h, hgrn_w_in[layer], lower_bounds[layer],
                              hgrn_gnorm_w[layer], hgrn_w_out[layer])
        else:
            j = layer - N_A_LAYERS
            mix = swa_sink_mixer(h, k_shared, v_shared, swa_w_q[j], swa_sinks[j],
                                 bias, mask, swa_w_out[j])
        h = layer_norm(ALPHA * h + mix, ln_mix_g[layer], ln_mix_b[layer])
        ff = conv_ffn(h, ffn_w_in[layer], ffn_conv_w[layer], ffn_conv_b[layer], ffn_w_out[layer])
        h = layer_norm(ALPHA * h + ff, ln_ffn_g[layer], ln_ffn_b[layer])
        if layer == N_A_LAYERS - 1:
            k_flat, v_flat = jnp.split(h @ shared_w_kv, 2, axis=-1)
            k_shared = k_flat.reshape(B, S, SW_KV_HEADS, SW_HEAD_DIM)
            v_shared = v_flat.reshape(B, S, SW_KV_HEADS, SW_HEAD_DIM)
    return h
```

```python
import functools

import numpy as np
import jax
import jax.numpy as jnp
from jax import lax
from jax.experimental import pallas as pl
from jax.experimental.pallas import tpu as pltpu

D_MODEL = 1024
DEPTH = 2
HG_HEADS = 8
HG_DIM = D_MODEL // HG_HEADS
SW_Q_HEADS = 16
SW_KV_HEADS = 4
SW_HEAD_DIM = D_MODEL // SW_Q_HEADS
SW_GROUP = SW_Q_HEADS // SW_KV_HEADS
SW_WINDOW = 128
KV_DIM = SW_KV_HEADS * SW_HEAD_DIM
REL_BUCKETS = 32
REL_MAX_DIST = 128
FFN_DIM = 2816
ALPHA = (2.0 * DEPTH) ** 0.25
LN_EPS = 1e-5
RMS_EPS = 1e-6

LANES = 128
SUBLANES = 8
HG_CHUNK = 128
HG_TILE = 512
FFN_TILE = 512
FFN_COLS = 256
SWA_TILE = 256
VMEM_LIMIT = 56 * 1024 * 1024

F32 = jnp.float32
BF16 = jnp.bfloat16


def _layer_norm(y, g, b):
    mu = jnp.mean(y, axis=-1, keepdims=True)
    yc = y - mu
    var = jnp.mean(yc * yc, axis=-1, keepdims=True)
    return yc * lax.rsqrt(var + LN_EPS) * g + b


def _silu(x):
    return x * jax.nn.sigmoid(x)


def _dot(a, b):
    return jnp.dot(a, b, preferred_element_type=F32)


def _dot_nt(a, b):
    return lax.dot_general(a, b, (((1,), (1,)), ((), ())), preferred_element_type=F32)


def _dot_tn(a, b):
    return lax.dot_general(a, b, (((0,), (0,)), ((), ())), preferred_element_type=F32)


HG_LEVEL_BLOCKS = (0, 1, 2, 4, 8, 16, 32, 64)


def _hgrn_constants():
    t = np.arange(HG_CHUNK)[:, None]
    s = np.arange(HG_CHUNK)[None, :]
    x = t ^ s
    lvl = np.where(x == 0, 0, np.floor(np.log2(np.maximum(x, 1))).astype(np.int64) + 1)
    lvl = np.where(s <= t, lvl, -1).astype(np.int32)
    tri = (s <= t).astype(np.float32)
    return jnp.asarray(tri, dtype=BF16), jnp.asarray(lvl)


def _level_reference(b, m):
    c, n = b.shape
    if m >= SUBLANES // 2:
        b3 = b.reshape(c // (2 * m), 2 * m, n)
        ref = jnp.broadcast_to(b3[:, m - 1:m, :], b3.shape)
        return ref.reshape(c, n)
    b3 = b.reshape(c // SUBLANES, SUBLANES, n)
    sub = lax.broadcasted_iota(jnp.int32, b3.shape, 1)
    ref = jnp.broadcast_to(b3[:, m - 1:m, :], b3.shape)
    for start in range(2 * m, SUBLANES, 2 * m):
        row = jnp.broadcast_to(b3[:, start + m - 1:start + m, :], b3.shape)
        ref = jnp.where(sub >= start, row, ref)
    return ref.reshape(c, n)


def _hgrn_kernel(x_ref, win_ref, lbl_ref, gw_ref, wout_ref, lng_ref, lnb_ref, tri_ref, lvl_ref,
                 o_ref, proj_ref, og_ref, state_ref):
    tile = x_ref.shape[1]
    d = D_MODEL

    @pl.when(pl.program_id(1) == 0)
    def _():
        state_ref[...] = jnp.zeros_like(state_ref)

    xt = x_ref[0]
    xb = xt.astype(BF16)
    for grp in range(4):
        proj_ref[:, grp * d:(grp + 1) * d] = _dot(xb, win_ref[:, grp * d:(grp + 1) * d])

    lbl = lbl_ref[...]
    e = jnp.exp(lbl - jnp.max(lbl, axis=0, keepdims=True))
    lb = e[0:1] / jnp.sum(e, axis=0, keepdims=True)
    gw = gw_ref[...]
    tri = tri_ref[...]
    lvl = lvl_ref[...]

    def chunk_body(c, carry):
        r0 = pl.multiple_of(c * HG_CHUNK, HG_CHUNK)
        rows = pl.ds(r0, HG_CHUNK)
        f = proj_ref[rows, d:2 * d]
        fg = lb + (1.0 - lb) * jax.nn.sigmoid(f)
        logf = jnp.log(fg)
        k_all = 1.0 - fg
        hi = logf.astype(BF16)
        r1 = logf - hi.astype(F32)
        mid = r1.astype(BF16)
        lo = (r1 - mid.astype(F32)).astype(BF16)
        b_all = _dot(tri, hi) + _dot(tri, mid) + _dot(tri, lo)

        for h in range(HG_HEADS):
            col = slice(h * HG_DIM, (h + 1) * HG_DIM)
            q = _silu(proj_ref[rows, h * HG_DIM:(h + 1) * HG_DIM])
            v = proj_ref[rows, 2 * d + h * HG_DIM:2 * d + (h + 1) * HG_DIM]
            g = proj_ref[rows, 3 * d + h * HG_DIM:3 * d + (h + 1) * HG_DIM]
            k = k_all[:, col]
            b = b_all[:, col]
            vb = v.astype(BF16)

            scores = jnp.zeros((HG_CHUNK, HG_CHUNK), F32)
            for level, m in enumerate(HG_LEVEL_BLOCKS):
                if m == 0:
                    ql, kl = q, k
                else:
                    ex = jnp.exp(-jnp.abs(b - _level_reference(b, m)))
                    ql, kl = q * ex, k * ex
                s_l = _dot_nt(ql.astype(BF16), kl.astype(BF16))
                scores = jnp.where(lvl == level, s_l, scores)

            b_last = b[HG_CHUNK - 1:HG_CHUNK, :]
            st = state_ref[h]
            o = _dot_nt((q * jnp.exp(b)).astype(BF16), st.astype(BF16)) + _dot(scores.astype(BF16), vb)
            k_dec = (k * jnp.exp(b_last - b)).astype(BF16)
            state_ref[h] = st * jnp.exp(b_last) + _dot_tn(vb, k_dec)

            o = o * lax.rsqrt(jnp.mean(o * o, axis=-1, keepdims=True) + RMS_EPS)
            o = o * gw * _silu(g)
            og_ref[rows, col] = o.astype(BF16)
        return carry

    lax.fori_loop(0, tile // HG_CHUNK, chunk_body, 0)

    mix = _dot(og_ref[...], wout_ref[...])
    o_ref[0] = _layer_norm(ALPHA * xt + mix, lng_ref[...], lnb_ref[...])


def _resident():
    return pl.BlockSpec(memory_space=pltpu.VMEM)


def _hgrn_layer(x, w_in, lb_logits, gnorm_w, w_out, ln_g, ln_b):
    bsz, seq, d = x.shape
    tile = min(HG_TILE, seq)
    tri, lvl = _hgrn_constants()
    tok = pl.BlockSpec((1, tile, d), lambda b, j: (b, j, 0))
    return pl.pallas_call(
        _hgrn_kernel,
        grid=(bsz, seq // tile),
        in_specs=[tok] + [_resident()] * 8,
        out_specs=tok,
        out_shape=jax.ShapeDtypeStruct(x.shape, x.dtype),
        scratch_shapes=[
            pltpu.VMEM((tile, 4 * d), F32),
            pltpu.VMEM((tile, d), BF16),
            pltpu.VMEM((HG_HEADS, HG_DIM, HG_DIM), F32),
        ],
        compiler_params=pltpu.CompilerParams(
            dimension_semantics=("arbitrary", "arbitrary"), vmem_limit_bytes=VMEM_LIMIT),
        name="hgrn2_mixer",
    )(x, w_in.astype(BF16), lb_logits, gnorm_w.reshape(1, HG_DIM), w_out.astype(BF16),
      ln_g.reshape(1, d), ln_b.reshape(1, d), tri, lvl)


def _ffn_kernel(*refs, with_kv):
    if with_kv:
        (h_ref, win_ref, cw_ref, cb_ref, wout_ref, lng_ref, lnb_ref, wkv_ref,
         o_ref, k_ref, v_ref, acc_ref, carry_ref) = refs
    else:
        (h_ref, win_ref, cw_ref, cb_ref, wout_ref, lng_ref, lnb_ref,
         o_ref, acc_ref, carry_ref) = refs
    tile = h_ref.shape[1]
    n_chunks = win_ref.shape[1]
    cols = win_ref.shape[3]

    @pl.when(pl.program_id(1) == 0)
    def _():
        carry_ref[...] = jnp.zeros_like(carry_ref)

    h = h_ref[0]
    hb = h.astype(BF16)
    row = lax.broadcasted_iota(jnp.int32, (tile, cols), 0)

    for c in range(n_chunks):
        halves = []
        for half in range(2):
            u = _dot(hb, win_ref[half, c])
            prev = carry_ref[half, c]
            p1 = prev[SUBLANES - 1:SUBLANES]
            p2 = prev[SUBLANES - 2:SUBLANES - 1]
            um1 = jnp.where(row == 0, p1, pltpu.roll(u, 1, 0))
            um2 = jnp.where(row == 0, p2, jnp.where(row == 1, p1, pltpu.roll(u, 2, 0)))
            cw = cw_ref[half, c]
            halves.append(cw[0:1] * um2 + cw[1:2] * um1 + cw[2:3] * u + cb_ref[half, c])
            carry_ref[half, c] = u[tile - SUBLANES:tile]
        act = (_silu(halves[0]) * halves[1]).astype(BF16)
        part = _dot(act, wout_ref[c])
        if c == 0:
            acc_ref[...] = part
        else:
            acc_ref[...] += part

    out = _layer_norm(ALPHA * h + acc_ref[...], lng_ref[...], lnb_ref[...])
    o_ref[0] = out
    if with_kv:
        kv = _dot(out.astype(BF16), wkv_ref[...])
        k_ref[0] = kv[:, :KV_DIM].astype(BF16)
        v_ref[0] = kv[:, KV_DIM:].astype(BF16)


def _ffn_layer(h, w_in, conv_w, conv_b, w_out, ln_g, ln_b, w_kv=None):
    bsz, seq, d = h.shape
    f = w_out.shape[0]
    tile = min(FFN_TILE, seq)
    cols = FFN_COLS
    n_chunks = f // cols
    win = w_in.astype(BF16).reshape(d, 2, n_chunks, cols).transpose(1, 2, 0, 3)
    cw = conv_w.reshape(3, 2, n_chunks, cols).transpose(1, 2, 0, 3)
    cb = conv_b.reshape(2, n_chunks, 1, cols)
    wout = w_out.astype(BF16).reshape(n_chunks, cols, d)
    with_kv = w_kv is not None
    tok = pl.BlockSpec((1, tile, d), lambda b, j: (b, j, 0))
    args = [h, win, cw, cb, wout, ln_g.reshape(1, d), ln_b.reshape(1, d)]
    out_shape = [jax.ShapeDtypeStruct(h.shape, h.dtype)]
    out_specs = [tok]
    if with_kv:
        args.append(w_kv.astype(BF16))
        kv_spec = pl.BlockSpec((1, tile, KV_DIM), lambda b, j: (b, j, 0))
        out_shape += [jax.ShapeDtypeStruct((bsz, seq, KV_DIM), BF16)] * 2
        out_specs += [kv_spec, kv_spec]
    res = pl.pallas_call(
        functools.partial(_ffn_kernel, with_kv=with_kv),
        grid=(bsz, seq // tile),
        in_specs=[tok] + [_resident()] * (len(args) - 1),
        out_specs=out_specs,
        out_shape=out_shape,
        scratch_shapes=[
            pltpu.VMEM((tile, d), F32),
            pltpu.VMEM((2, n_chunks, SUBLANES, cols), F32),
        ],
        compiler_params=pltpu.CompilerParams(
            dimension_semantics=("arbitrary", "arbitrary"), vmem_limit_bytes=VMEM_LIMIT),
        name="conv_ffn_kv" if with_kv else "conv_ffn",
    )(*args)
    return res if with_kv else res[0]


def _t5_causal_bucket(dist):
    exact = REL_BUCKETS // 2
    dd = jnp.maximum(dist, 1).astype(F32)
    log_b = exact + (jnp.log(dd / exact) / np.log(REL_MAX_DIST / exact)
                     * (REL_BUCKETS - exact)).astype(jnp.int32)
    return jnp.where(dist < exact, dist, jnp.minimum(log_b, REL_BUCKETS - 1))


def _band_geometry():
    t = jnp.arange(SW_WINDOW)[:, None] + SW_WINDOW
    s = jnp.arange(2 * SW_WINDOW)[None, :]
    dist = t - s
    band = (dist >= 0) & (dist < SW_WINDOW)
    bucket = jnp.where(band, _t5_causal_bucket(jnp.maximum(dist, 0)), -1)
    first = jnp.where(s >= SW_WINDOW, bucket, -1)
    return jnp.stack([bucket, first]).astype(jnp.int32)


def _swa_kernel(h_ref, kc_ref, kp_ref, vc_ref, vp_ref, wq_ref, wo_ref, sink_ref, rel_ref, geo_ref,
                lng_ref, lnb_ref, o_ref, bias_ref, attn_ref):
    tile = h_ref.shape[1]
    w = SW_WINDOW
    j = pl.program_id(1)

    @pl.when((pl.program_id(0) == 0) & (j == 0))
    def _():
        for variant in range(2):
            def head_body(hq, carry, variant=variant):
                geo = geo_ref[variant]
                acc = jnp.full(geo.shape, -jnp.inf, F32)
                for bucket in range(REL_BUCKETS):
                    acc = jnp.where(geo == bucket, rel_ref[bucket, hq], acc)
                bias_ref[variant, hq] = acc
                return carry
            lax.fori_loop(0, SW_Q_HEADS, head_body, 0)

    h = h_ref[0]
    scale = SW_HEAD_DIM ** -0.5
    qb = (_dot(h.astype(BF16), wq_ref[...]) * scale).astype(BF16)
    kfull = jnp.concatenate([kp_ref[0], kc_ref[0]], axis=0)
    vfull = jnp.concatenate([vp_ref[0], vc_ref[0]], axis=0)
    lane = lax.broadcasted_iota(jnp.int32, (2 * w, LANES), 1)
    lane_q = lax.broadcasted_iota(jnp.int32, (w, LANES), 1)
    low = lane < SW_HEAD_DIM
    zero = jnp.zeros((2 * w, LANES), BF16)

    def both_halves(col, g):
        if g % 2 == 0:
            lo = jnp.where(low, col, zero)
            hi = pltpu.roll(lo, SW_HEAD_DIM, 1)
        else:
            hi = jnp.where(low, zero, col)
            lo = pltpu.roll(hi, SW_HEAD_DIM, 1)
        return jnp.concatenate([lo, hi], axis=0)

    for i in range(tile // w):
        kk = kfull[i * w:(i + 2) * w]
        vv = vfull[i * w:(i + 2) * w]
        variant = jnp.where(j == 0, 1, 0) if i == 0 else 0
        for g in range(SW_KV_HEADS):
            cs = slice((g // 2) * LANES, (g // 2 + 1) * LANES)
            kblk = both_halves(kk[:, cs], g)
            vblk = both_halves(vv[:, cs], g)
            for pair in range(SW_GROUP // 2):
                qc = (g * SW_GROUP) // 2 + pair
                logits2 = _dot_nt(qb[i * w:(i + 1) * w, qc * LANES:(qc + 1) * LANES], kblk)
                probs, invs = [], []
                for e_ in range(2):
                    hq = 2 * qc + e_
                    lg = logits2[:, e_ * 2 * w:(e_ + 1) * 2 * w] + bias_ref[variant, hq]
                    sink = sink_ref[hq]
                    m = jnp.maximum(jnp.max(lg, axis=-1, keepdims=True), sink)
                    p = jnp.exp(lg - m)
                    den = jnp.sum(p, axis=-1, keepdims=True) + jnp.exp(sink - m)
                    probs.append(p.astype(BF16))
                    invs.append(1.0 / den)
                o2 = _dot(jnp.concatenate(probs, axis=1), vblk)
                o2 = o2 * jnp.where(lane_q < SW_HEAD_DIM, invs[0], invs[1])
                attn_ref[i * w:(i + 1) * w, qc * LANES:(qc + 1) * LANES] = o2.astype(BF16)

    mix = _dot(attn_ref[...], wo_ref[...])
    o_ref[0] = _layer_norm(ALPHA * h + mix, lng_ref[...], lnb_ref[...])


def _swa_layer(h, k, v, w_q, sinks, rel_bias, w_out, ln_g, ln_b):
    bsz, seq, d = h.shape
    tile = min(SWA_TILE, seq)
    w = SW_WINDOW
    bpt = tile // w
    tok = pl.BlockSpec((1, tile, d), lambda b, j: (b, j, 0))
    cur = pl.BlockSpec((1, tile, KV_DIM), lambda b, j: (b, j, 0))
    prev = pl.BlockSpec((1, w, KV_DIM), lambda b, j: (b, jnp.maximum(j * bpt - 1, 0), 0))
    smem = pl.BlockSpec(memory_space=pltpu.SMEM)
    return pl.pallas_call(
        _swa_kernel,
        grid=(bsz, seq // tile),
        in_specs=[tok, cur, prev, cur, prev, _resident(), _resident(), smem, smem, _resident(),
                  _resident(), _resident()],
        out_specs=tok,
        out_shape=jax.ShapeDtypeStruct(h.shape, h.dtype),
        scratch_shapes=[
            pltpu.VMEM((2, SW_Q_HEADS, w, 2 * w), F32),
            pltpu.VMEM((tile, d), BF16),
        ],
        compiler_params=pltpu.CompilerParams(
            dimension_semantics=("arbitrary", "arbitrary"), vmem_limit_bytes=VMEM_LIMIT),
        name="swa_sink_mixer",
    )(h, k, k, v, v, w_q.astype(BF16), w_out.astype(BF16), sinks, rel_bias, _band_geometry(),
      ln_g.reshape(1, d), ln_b.reshape(1, d))


def kernel(x, hgrn_w_in, hgrn_lb_logits, hgrn_gnorm_w, hgrn_w_out, swa_w_q, swa_sinks, swa_w_out,
           shared_w_kv, rel_bias, ffn_w_in, ffn_conv_w, ffn_conv_b, ffn_w_out,
           ln_mix_g, ln_mix_b, ln_ffn_g, ln_ffn_b):
    h = _hgrn_layer(x, hgrn_w_in[0], hgrn_lb_logits, hgrn_gnorm_w[0], hgrn_w_out[0],
                    ln_mix_g[0], ln_mix_b[0])
    h, k, v = _ffn_layer(h, ffn_w_in[0], ffn_conv_w[0], ffn_conv_b[0], ffn_w_out[0],
                         ln_ffn_g[0], ln_ffn_b[0], w_kv=shared_w_kv)
    h = _swa_layer(h, k, v, swa_w_q[0], swa_sinks[0], rel_bias, swa_w_out[0],
                   ln_mix_g[1], ln_mix_b[1])
    h = _ffn_layer(h, ffn_w_in[1], ffn_conv_w[1], ffn_conv_b[1], ffn_w_out[1],
                   ln_ffn_g[1], ln_ffn_b[1])
    return h
```

```python
import functools

import numpy as np
import jax
import jax.numpy as jnp
from jax import lax
from jax.experimental import pallas as pl
from jax.experimental.pallas import tpu as pltpu

D_MODEL = 1024
DEPTH = 2
HG_HEADS = 8
HG_DIM = D_MODEL // HG_HEADS
SW_Q_HEADS = 16
SW_KV_HEADS = 4
SW_HEAD_DIM = D_MODEL // SW_Q_HEADS
SW_GROUP = SW_Q_HEADS // SW_KV_HEADS
SW_WINDOW = 128
KV_DIM = SW_KV_HEADS * SW_HEAD_DIM
REL_BUCKETS = 32
REL_MAX_DIST = 128
FFN_DIM = 2816
ALPHA = (2.0 * DEPTH) ** 0.25
LN_EPS = 1e-5
RMS_EPS = 1e-6

LANES = 128
SUBLANES = 8
HG_CHUNK = 128
HG_TILE = 512
FFN_TILE = 512
FFN_COLS = 256
FFN_OUT_BLOCKS = 1
SWA_TILE = 256
VMEM_LIMIT = 56 * 1024 * 1024

F32 = jnp.float32
BF16 = jnp.bfloat16


def _layer_norm(y, g, b):
    mu = jnp.mean(y, axis=-1, keepdims=True)
    yc = y - mu
    var = jnp.mean(yc * yc, axis=-1, keepdims=True)
    return yc * lax.rsqrt(var + LN_EPS) * g + b


def _silu(x):
    return x * jax.nn.sigmoid(x)


def _dot(a, b):
    return jnp.dot(a, b, preferred_element_type=F32)


def _dot_nt(a, b):
    return lax.dot_general(a, b, (((1,), (1,)), ((), ())), preferred_element_type=F32)


def _dot_tn(a, b):
    return lax.dot_general(a, b, (((0,), (0,)), ((), ())), preferred_element_type=F32)


HG_LEVEL_BLOCKS = (0, 1, 2, 4, 8, 16, 32, 64)
HG_MERGE = 16
HG_MAX_EXPONENT = 120.0
LOG2E = 1.4426950408889634


def _hgrn_constants():
    t = np.arange(HG_CHUNK)[:, None]
    s = np.arange(HG_CHUNK)[None, :]
    x = t ^ s
    lvl = np.where(x == 0, 0, np.floor(np.log2(np.maximum(x, 1))).astype(np.int64) + 1)
    lvl = np.where(s <= t, lvl, -1).astype(np.int32)
    tri = (s <= t).astype(np.float32)
    return jnp.asarray(tri, dtype=BF16), jnp.asarray(lvl)


def _level_reference(b, m):
    c, n = b.shape
    if m >= SUBLANES // 2:
        b3 = b.reshape(c // (2 * m), 2 * m, n)
        ref = jnp.broadcast_to(b3[:, m - 1:m, :], b3.shape)
        return ref.reshape(c, n)
    b3 = b.reshape(c // SUBLANES, SUBLANES, n)
    sub = lax.broadcasted_iota(jnp.int32, b3.shape, 1)
    ref = jnp.broadcast_to(b3[:, m - 1:m, :], b3.shape)
    for start in range(2 * m, SUBLANES, 2 * m):
        row = jnp.broadcast_to(b3[:, start + m - 1:start + m, :], b3.shape)
        ref = jnp.where(sub >= start, row, ref)
    return ref.reshape(c, n)


def _block_start(b, n_rows):
    c, n = b.shape
    b3 = b.reshape(c // n_rows, n_rows, n)
    return jnp.broadcast_to(b3[:, 0:1, :], b3.shape).reshape(c, n)


def _neg_abs(x):
    bits = lax.bitcast_convert_type(x, jnp.uint32) | jnp.uint32(0x80000000)
    return lax.bitcast_convert_type(bits, F32)


def _hgrn_kernel(x_ref, win_ref, lbl_ref, gw_ref, wout_ref, lng_ref, lnb_ref, tri_ref, lvl_ref,
                 o_ref, proj_ref, og_ref, state_ref):
    tile = x_ref.shape[1]
    d = D_MODEL

    @pl.when(pl.program_id(1) == 0)
    def _():
        state_ref[...] = jnp.zeros_like(state_ref)

    xt = x_ref[0]
    xb = xt.astype(BF16)
    for grp in range(4):
        proj_ref[:, grp * d:(grp + 1) * d] = _dot(xb, win_ref[:, grp * d:(grp + 1) * d])

    lbl = lbl_ref[...]
    e = jnp.exp(lbl - jnp.max(lbl, axis=0, keepdims=True))
    lb = e[0:1] / jnp.sum(e, axis=0, keepdims=True)
    gw = gw_ref[...]
    tri = tri_ref[...]
    lvl = lvl_ref[...]
    mergeable = jnp.min(jnp.log2(lb)) * (HG_MERGE - 1) > -HG_MAX_EXPONENT

    def chunk_body(c, carry, merged):
        r0 = pl.multiple_of(c * HG_CHUNK, HG_CHUNK)
        rows = pl.ds(r0, HG_CHUNK)
        f = proj_ref[rows, d:2 * d]
        fg = lb + (1.0 - lb) * jax.nn.sigmoid(f)
        logf = jnp.log(fg)
        k_all = 1.0 - fg
        hi = logf.astype(BF16)
        r1 = logf - hi.astype(F32)
        mid = r1.astype(BF16)
        lo = (r1 - mid.astype(F32)).astype(BF16)
        b_all = (_dot(tri, hi) + _dot(tri, mid) + _dot(tri, lo)) * LOG2E

        for h in range(HG_HEADS):
            col = slice(h * HG_DIM, (h + 1) * HG_DIM)
            q = _silu(proj_ref[rows, h * HG_DIM:(h + 1) * HG_DIM])
            v = proj_ref[rows, 2 * d + h * HG_DIM:2 * d + (h + 1) * HG_DIM]
            g = proj_ref[rows, 3 * d + h * HG_DIM:3 * d + (h + 1) * HG_DIM]
            k = k_all[:, col]
            b = b_all[:, col]
            qb = q.astype(BF16)
            kb = k.astype(BF16)
            vb = v.astype(BF16)

            if merged:
                dlt = b - _block_start(b, HG_MERGE)
                scores = _dot_nt(qb * jnp.exp2(dlt).astype(BF16), kb * jnp.exp2(-dlt).astype(BF16))
                scores = jnp.where(lvl >= 0, scores, 0.0)
                first = HG_LEVEL_BLOCKS.index(HG_MERGE)
            else:
                scores = jnp.where(lvl == 0, _dot_nt(qb, kb), 0.0)
                first = 1
            for level in range(first, len(HG_LEVEL_BLOCKS)):
                m = HG_LEVEL_BLOCKS[level]
                ex = jnp.exp2(_neg_abs(b - _level_reference(b, m))).astype(BF16)
                scores = jnp.where(lvl == level, _dot_nt(qb * ex, kb * ex), scores)

            b_last = b[HG_CHUNK - 1:HG_CHUNK, :]
            st = state_ref[h]
            o = _dot_nt((q * jnp.exp2(b)).astype(BF16), st.astype(BF16)) + _dot(scores.astype(BF16), vb)
            k_dec = (k * jnp.exp2(b_last - b)).astype(BF16)
            state_ref[h] = st * jnp.exp2(b_last) + _dot_tn(vb, k_dec)

            o = o * lax.rsqrt(jnp.mean(o * o, axis=-1, keepdims=True) + RMS_EPS)
            o = o * gw * _silu(g)
            og_ref[rows, col] = o.astype(BF16)
        return carry

    n_chunks = tile // HG_CHUNK

    @pl.when(mergeable)
    def _():
        lax.fori_loop(0, n_chunks, functools.partial(chunk_body, merged=True), 0)

    @pl.when(jnp.logical_not(mergeable))
    def _():
        lax.fori_loop(0, n_chunks, functools.partial(chunk_body, merged=False), 0)

    mix =_dot(og_ref[...], wout_ref[...])
    o_ref[0] = _layer_norm(ALPHA * xt + mix, lng_ref[...], lnb_ref[...])


def _resident():
    return pl.BlockSpec(memory_space=pltpu.VMEM)


def _hgrn_layer(x, w_in, lb_logits, gnorm_w, w_out, ln_g, ln_b):
    bsz, seq, d = x.shape
    tile = min(HG_TILE, seq)
    tri, lvl = _hgrn_constants()
    tok = pl.BlockSpec((1, tile, d), lambda b, j: (b, j, 0))
    return pl.pallas_call(
        _hgrn_kernel,
        grid=(bsz, seq // tile),
        in_specs=[tok] + [_resident()] * 8,
        out_specs=tok,
        out_shape=jax.ShapeDtypeStruct(x.shape, x.dtype),
        scratch_shapes=[
            pltpu.VMEM((tile, 4 * d), F32),
            pltpu.VMEM((tile, d), BF16),
            pltpu.VMEM((HG_HEADS, HG_DIM, HG_DIM), F32),
        ],
        compiler_params=pltpu.CompilerParams(
            dimension_semantics=("arbitrary", "arbitrary"), vmem_limit_bytes=VMEM_LIMIT),
        name="hgrn2_mixer",
    )(x, w_in.astype(BF16), lb_logits, gnorm_w.reshape(1, HG_DIM), w_out.astype(BF16),
      ln_g.reshape(1, d), ln_b.reshape(1, d), tri, lvl)


def _ffn_kernel(*refs, with_kv):
    if with_kv:
        (h_ref, win_ref, cw_ref, cb_ref, wout_ref, lng_ref, lnb_ref, wkv_ref,
         o_ref, k_ref, v_ref, u_ref, act_ref, carry_ref) = refs
    else:
        (h_ref, win_ref, cw_ref, cb_ref, wout_ref, lng_ref, lnb_ref,
         o_ref, u_ref, act_ref, carry_ref) = refs
    tile = h_ref.shape[1]
    f = wout_ref.shape[0]
    cols = FFN_COLS

    @pl.when(pl.program_id(1) == 0)
    def _():
        carry_ref[...] = jnp.zeros_like(carry_ref)

    h = h_ref[0]
    hb = h.astype(BF16)
    row8 = lax.broadcasted_iota(jnp.int32, (SUBLANES, cols), 0)

    def conv(cs):
        u = u_ref[:, cs]
        prev = carry_ref[:, cs]
        carry_ref[:, cs] = u[tile - SUBLANES:tile]
        p1 = prev[SUBLANES - 1:SUBLANES]
        p2 = prev[SUBLANES - 2:SUBLANES - 1]
        r1 = pltpu.roll(u, 1, 0)
        r2 = pltpu.roll(u, 2, 0)
        top1 = jnp.where(row8 == 0, p1, r1[0:SUBLANES])
        top2 = jnp.where(row8 == 0, p2, jnp.where(row8 == 1, p1, r2[0:SUBLANES]))
        um1 = jnp.concatenate([top1, r1[SUBLANES:]], axis=0)
        um2 = jnp.concatenate([top2, r2[SUBLANES:]], axis=0)
        return cw_ref[0:1, cs] * um2 + cw_ref[1:2, cs] * um1 + cw_ref[2:3, cs] * u + cb_ref[:, cs]

    for c in range(f // cols):
        ga = slice(c * cols, (c + 1) * cols)
        gb = slice(f + c * cols, f + (c + 1) * cols)
        u_ref[:, ga] = _dot(hb, win_ref[:, ga])
        u_ref[:, gb] = _dot(hb, win_ref[:, gb])
        act_ref[:, ga] = (_silu(conv(ga)) * conv(gb)).astype(BF16)

    blk = tile // FFN_OUT_BLOCKS
    for r in range(FFN_OUT_BLOCKS):
        rows = slice(r * blk, (r + 1) * blk)
        y = ALPHA * h_ref[0, rows, :] + _dot(act_ref[rows, :], wout_ref[...])
        out = _layer_norm(y, lng_ref[...], lnb_ref[...])
        o_ref[0, rows, :] = out
        if with_kv:
            kv = _dot(out.astype(BF16), wkv_ref[...])
            k_ref[0, rows, :] = kv[:, :KV_DIM].astype(BF16)
            v_ref[0, rows, :] = kv[:, KV_DIM:].astype(BF16)


def _ffn_layer(h, w_in, conv_w, conv_b, w_out, ln_g, ln_b, w_kv=None):
    bsz, seq, d = h.shape
    f = w_out.shape[0]
    tile = min(FFN_TILE, seq)
    with_kv = w_kv is not None
    tok = pl.BlockSpec((1, tile, d), lambda b, j: (b, j, 0))
    args = [h, w_in.astype(BF16), conv_w, conv_b.reshape(1, 2 * f), w_out.astype(BF16),
            ln_g.reshape(1, d), ln_b.reshape(1, d)]
    out_shape = [jax.ShapeDtypeStruct(h.shape, h.dtype)]
    out_specs = [tok]
    if with_kv:
        args.append(w_kv.astype(BF16))
        kv_spec = pl.BlockSpec((1, tile, KV_DIM), lambda b, j: (b, j, 0))
        out_shape += [jax.ShapeDtypeStruct((bsz, seq, KV_DIM), BF16)] * 2
        out_specs += [kv_spec, kv_spec]
    res = pl.pallas_call(
        functools.partial(_ffn_kernel, with_kv=with_kv),
        grid=(bsz, seq // tile),
        in_specs=[tok] + [_resident()] * (len(args) - 1),
        out_specs=out_specs,
        out_shape=out_shape,
        scratch_shapes=[
            pltpu.VMEM((tile, 2 * f), F32),
            pltpu.VMEM((tile, f), BF16),
            pltpu.VMEM((SUBLANES, 2 * f), F32),
        ],
        compiler_params=pltpu.CompilerParams(
            dimension_semantics=("arbitrary", "arbitrary"), vmem_limit_bytes=VMEM_LIMIT),
        name="conv_ffn_kv" if with_kv else "conv_ffn",
    )(*args)
    return res if with_kv else res[0]


def _t5_causal_bucket(dist):
    exact = REL_BUCKETS // 2
    dd = jnp.maximum(dist, 1).astype(F32)
    log_b = exact + (jnp.log(dd / exact) / np.log(REL_MAX_DIST / exact)
                     * (REL_BUCKETS - exact)).astype(jnp.int32)
    return jnp.where(dist < exact, dist, jnp.minimum(log_b, REL_BUCKETS - 1))


def _band_geometry():
    t = jnp.arange(SW_WINDOW)[:, None] + SW_WINDOW
    s = jnp.arange(2 * SW_WINDOW)[None, :]
    dist = t - s
    band = (dist >= 0) & (dist < SW_WINDOW)
    bucket = jnp.where(band, _t5_causal_bucket(jnp.maximum(dist, 0)), -1)
    first = jnp.where(s >= SW_WINDOW, bucket, -1)
    return jnp.stack([bucket, first]).astype(jnp.int32)


def _swa_kernel(h_ref, kc_ref, kp_ref, vc_ref, vp_ref, wq_ref, wo_ref, sink_ref, rel_ref, geo_ref,
                lng_ref, lnb_ref, o_ref, bias_ref, attn_ref):
    tile = h_ref.shape[1]
    w = SW_WINDOW
    j = pl.program_id(1)

    @pl.when((pl.program_id(0) == 0) & (j == 0))
    def _():
        for variant in range(2):
            def head_body(hq, carry, variant=variant):
                geo = geo_ref[variant]
                acc = jnp.full(geo.shape, -jnp.inf, F32)
                for bucket in range(REL_BUCKETS):
                    acc = jnp.where(geo == bucket, rel_ref[bucket, hq], acc)
                bias_ref[variant, hq] = acc
                return carry
            lax.fori_loop(0, SW_Q_HEADS, head_body, 0)

    h = h_ref[0]
    scale = SW_HEAD_DIM ** -0.5
    qb = (_dot(h.astype(BF16), wq_ref[...]) * scale).astype(BF16)
    kfull = jnp.concatenate([kp_ref[0], kc_ref[0]], axis=0)
    vfull = jnp.concatenate([vp_ref[0], vc_ref[0]], axis=0)
    lane = lax.broadcasted_iota(jnp.int32, (2 * w, LANES), 1)
    lane_q = lax.broadcasted_iota(jnp.int32, (w, LANES), 1)
    low = lane < SW_HEAD_DIM
    zero = jnp.zeros((2 * w, LANES), BF16)

    def both_halves(col, g):
        if g % 2 == 0:
            lo = jnp.where(low, col, zero)
            hi = pltpu.roll(lo, SW_HEAD_DIM, 1)
        else:
            hi = jnp.where(low, zero, col)
            lo = pltpu.roll(hi, SW_HEAD_DIM, 1)
        return jnp.concatenate([lo, hi], axis=0)

    for i in range(tile // w):
        kk = kfull[i * w:(i + 2) * w]
        vv = vfull[i * w:(i + 2) * w]
        variant = jnp.where(j == 0, 1, 0) if i == 0 else 0
        for g in range(SW_KV_HEADS):
            cs = slice((g // 2) * LANES, (g // 2 + 1) * LANES)
            kblk = both_halves(kk[:, cs], g)
            vblk = both_halves(vv[:, cs], g)
            for pair in range(SW_GROUP // 2):
                qc = (g * SW_GROUP) // 2 + pair
                logits2 = _dot_nt(qb[i * w:(i + 1) * w, qc * LANES:(qc + 1) * LANES], kblk)
                probs, invs = [], []
                for e_ in range(2):
                    hq = 2 * qc + e_
                    lg = logits2[:, e_ * 2 * w:(e_ + 1) * 2 * w] + bias_ref[variant, hq]
                    sink = sink_ref[hq]
                    m = jnp.maximum(jnp.max(lg, axis=-1, keepdims=True), sink)
                    p = jnp.exp(lg - m)
                    den = jnp.sum(p, axis=-1, keepdims=True) + jnp.exp(sink - m)
                    probs.append(p.astype(BF16))
                    invs.append(1.0 / den)
                o2 = _dot(jnp.concatenate(probs, axis=1), vblk)
                o2 = o2 * jnp.where(lane_q < SW_HEAD_DIM, invs[0], invs[1])
                attn_ref[i * w:(i + 1) * w, qc * LANES:(qc + 1) * LANES] = o2.astype(BF16)

    mix = _dot(attn_ref[...], wo_ref[...])
    o_ref[0] = _layer_norm(ALPHA * h + mix, lng_ref[...], lnb_ref[...])


def _swa_layer(h, k, v, w_q, sinks, rel_bias, w_out, ln_g, ln_b):
    bsz, seq, d = h.shape
    tile = min(SWA_TILE, seq)
    w = SW_WINDOW
    bpt = tile // w
    tok = pl.BlockSpec((1, tile, d), lambda b, j: (b, j, 0))
    cur = pl.BlockSpec((1, tile, KV_DIM), lambda b, j: (b, j, 0))
    prev = pl.BlockSpec((1, w, KV_DIM), lambda b, j: (b, jnp.maximum(j * bpt - 1, 0), 0))
    smem = pl.BlockSpec(memory_space=pltpu.SMEM)
    return pl.pallas_call(
        _swa_kernel,
        grid=(bsz, seq // tile),
        in_specs=[tok, cur, prev, cur, prev, _resident(), _resident(), smem, smem, _resident(),
                  _resident(), _resident()],
        out_specs=tok,
        out_shape=jax.ShapeDtypeStruct(h.shape, h.dtype),
        scratch_shapes=[
            pltpu.VMEM((2, SW_Q_HEADS, w, 2 * w), F32),
            pltpu.VMEM((tile, d), BF16),
        ],
        compiler_params=pltpu.CompilerParams(
            dimension_semantics=("arbitrary", "arbitrary"), vmem_limit_bytes=VMEM_LIMIT),
        name="swa_sink_mixer",
    )(h, k, k, v, v, w_q.astype(BF16), w_out.astype(BF16), sinks, rel_bias, _band_geometry(),
      ln_g.reshape(1, d), ln_b.reshape(1, d))


def kernel(x, hgrn_w_in, hgrn_lb_logits, hgrn_gnorm_w, hgrn_w_out, swa_w_q, swa_sinks, swa_w_out,
           shared_w_kv, rel_bias, ffn_w_in, ffn_conv_w, ffn_conv_b, ffn_w_out,
           ln_mix_g, ln_mix_b, ln_ffn_g, ln_ffn_b):
    h = _hgrn_layer(x, hgrn_w_in[0], hgrn_lb_logits, hgrn_gnorm_w[0], hgrn_w_out[0],
                    ln_mix_g[0], ln_mix_b[0])
    h, k, v = _ffn_layer(h, ffn_w_in[0], ffn_conv_w[0], ffn_conv_b[0], ffn_w_out[0],
                         ln_ffn_g[0], ln_ffn_b[0], w_kv=shared_w_kv)
    h = _swa_layer(h, k, v, swa_w_q[0], swa_sinks[0], rel_bias, swa_w_out[0],
                   ln_mix_g[1], ln_mix_b[1])
    h = _ffn_layer(h, ffn_w_in[1], ffn_conv_w[1], ffn_conv_b[1], ffn_w_out[1],
                   ln_ffn_g[1], ln_ffn_b[1])
    return h
```

```python
import functools

import numpy as np
import jax
import jax.numpy as jnp
from jax import lax
from jax.experimental import pallas as pl
from jax.experimental.pallas import tpu as pltpu

D_MODEL = 1024
DEPTH = 2
HG_HEADS = 8
HG_DIM = D_MODEL // HG_HEADS
SW_Q_HEADS = 16
SW_KV_HEADS = 4
SW_HEAD_DIM = D_MODEL // SW_Q_HEADS
SW_GROUP = SW_Q_HEADS // SW_KV_HEADS
SW_WINDOW = 128
KV_DIM = SW_KV_HEADS * SW_HEAD_DIM
REL_BUCKETS = 32
REL_MAX_DIST = 128
FFN_DIM = 2816
ALPHA = (2.0 * DEPTH) ** 0.25
LN_EPS = 1e-5
RMS_EPS = 1e-6

LANES = 128
SUBLANES = 8
HG_CHUNK = 128
HG_TILE = 512
HG_PROJ_COLS = 512
FFN_TILE = 512
FFN_COLS = 256
FFN_OUT_BLOCKS = 1
SWA_TILE = 256
VMEM_LIMIT = 56 * 1024 * 1024

F32 = jnp.float32
BF16 = jnp.bfloat16


def _layer_norm(y, g, b):
    mu = jnp.mean(y, axis=-1, keepdims=True)
    yc = y - mu
    var = jnp.mean(yc * yc, axis=-1, keepdims=True)
    return yc * lax.rsqrt(var + LN_EPS) * g + b


def _silu(x):
    return x * jax.nn.sigmoid(x)


def _dot(a, b):
    return jnp.dot(a, b, preferred_element_type=F32)


def _dot_nt(a, b):
    return lax.dot_general(a, b, (((1,), (1,)), ((), ())), preferred_element_type=F32)


def _dot_tn(a, b):
    return lax.dot_general(a, b, (((0,), (0,)), ((), ())), preferred_element_type=F32)


HG_LEVEL_BLOCKS = (0, 1, 2, 4, 8, 16, 32, 64)
HG_MERGE = 16
HG_MAX_EXPONENT = 120.0
LOG2E = 1.4426950408889634


def _hgrn_constants():
    t = np.arange(HG_CHUNK)[:, None]
    s = np.arange(HG_CHUNK)[None, :]
    x = t ^ s
    lvl = np.where(x == 0, 0, np.floor(np.log2(np.maximum(x, 1))).astype(np.int64) + 1)
    lvl = np.where(s <= t, lvl, -1).astype(np.int32)
    tri = (s <= t).astype(np.float32)
    return jnp.asarray(tri, dtype=BF16), jnp.asarray(lvl)


def _level_reference(b, m):
    c, n = b.shape
    if m >= SUBLANES // 2:
        b3 = b.reshape(c // (2 * m), 2 * m, n)
        ref = jnp.broadcast_to(b3[:, m - 1:m, :], b3.shape)
        return ref.reshape(c, n)
    b3 = b.reshape(c // SUBLANES, SUBLANES, n)
    sub = lax.broadcasted_iota(jnp.int32, b3.shape, 1)
    ref = jnp.broadcast_to(b3[:, m - 1:m, :], b3.shape)
    for start in range(2 * m, SUBLANES, 2 * m):
        row = jnp.broadcast_to(b3[:, start + m - 1:start + m, :], b3.shape)
        ref = jnp.where(sub >= start, row, ref)
    return ref.reshape(c, n)


def _block_start(b, n_rows):
    c, n = b.shape
    b3 = b.reshape(c // n_rows, n_rows, n)
    return jnp.broadcast_to(b3[:, 0:1, :], b3.shape).reshape(c, n)


def _neg_abs(x):
    bits = lax.bitcast_convert_type(x, jnp.uint32) | jnp.uint32(0x80000000)
    return lax.bitcast_convert_type(bits, F32)


def _hgrn_kernel(x_ref, win_ref, lbl_ref, gw_ref, wout_ref, lng_ref, lnb_ref, tri_ref, lvl_ref,
                 o_ref, proj_ref, og_ref, state_ref, o_scr):
    tile = x_ref.shape[1]
    d = D_MODEL

    @pl.when(pl.program_id(1) == 0)
    def _():
        state_ref[...] = jnp.zeros_like(state_ref)

    lbl = lbl_ref[...]
    e = jnp.exp(lbl - jnp.max(lbl, axis=0, keepdims=True))
    lb = e[0:1] / jnp.sum(e, axis=0, keepdims=True)
    gw = gw_ref[...]
    tri = tri_ref[...]
    lvl = lvl_ref[...]
    mergeable = jnp.min(jnp.log2(lb)) * (HG_MERGE - 1) > -HG_MAX_EXPONENT

    def project_part(rows, part):
        cols = slice(part * HG_PROJ_COLS, (part + 1) * HG_PROJ_COLS)
        proj_ref[rows, cols] = _dot(x_ref[0, rows, :].astype(BF16), win_ref[:, cols])

    def project(rows):
        for part in range(4 * d // HG_PROJ_COLS):
            project_part(rows, part)

    def finish(rows):
        mix = _dot(og_ref[rows, :], wout_ref[...])
        o_ref[0, rows, :] = _layer_norm(ALPHA * x_ref[0, rows, :] + mix, lng_ref[...], lnb_ref[...])

    def recurrence(rows, merged, side_jobs=()):
        fg = lb + (1.0 - lb) * jax.nn.sigmoid(proj_ref[rows, d:2 * d])
        logf = jnp.log(fg)
        k_all = 1.0 - fg
        q_all = _silu(proj_ref[rows, 0:d])
        hi = logf.astype(BF16)
        r1 = logf - hi.astype(F32)
        mid = r1.astype(BF16)
        lo = (r1 - mid.astype(F32)).astype(BF16)
        b_all = (_dot(tri, hi) + _dot(tri, mid) + _dot(tri, lo)) * LOG2E

        qb = q_all.astype(BF16)
        kb = k_all.astype(BF16)
        vb = proj_ref[rows, 2 * d:3 * d].astype(BF16)

        if merged:
            dlt = b_all - _block_start(b_all, HG_MERGE)
            pairs = [(None, qb * jnp.exp2(dlt).astype(BF16), kb * jnp.exp2(-dlt).astype(BF16))]
            first = HG_LEVEL_BLOCKS.index(HG_MERGE)
        else:
            pairs = [(0, qb, kb)]
            first = 1
        for level in range(first, len(HG_LEVEL_BLOCKS)):
            ex = jnp.exp2(_neg_abs(b_all - _level_reference(b_all, HG_LEVEL_BLOCKS[level]))).astype(BF16)
            pairs.append((level, qb * ex, kb * ex))
        b_last = b_all[HG_CHUNK - 1:HG_CHUNK, :]
        q_in = (q_all * jnp.exp2(b_all)).astype(BF16)
        k_dec = (k_all * jnp.exp2(b_last - b_all)).astype(BF16)
        decay = jnp.exp2(b_last)

        for h in range(HG_HEADS):
            col = slice(h * HG_DIM, (h + 1) * HG_DIM)
            scores = None
            for level, ql, kl in pairs:
                s_l = _dot_nt(ql[:, col], kl[:, col])
                if scores is None:
                    scores = jnp.where(lvl >= 0 if level is None else lvl == level, s_l, 0.0)
                else:
                    scores = jnp.where(lvl == level, s_l, scores)
            st = state_ref[h]
            o_scr[:, col] = _dot_nt(q_in[:, col], st.astype(BF16)) + _dot(scores.astype(BF16), vb[:, col])
            state_ref[h] = st * decay[:, col] + _dot_tn(vb[:, col], k_dec[:, col])
            for job in side_jobs[h::HG_HEADS]:
                job()

        o_all = o_scr[...]
        sq = o_all * o_all
        inv = [jnp.broadcast_to(lax.rsqrt(jnp.mean(sq[:, h * HG_DIM:(h + 1) * HG_DIM], axis=-1, keepdims=True)
                                          + RMS_EPS), (HG_CHUNK, HG_DIM)) for h in range(HG_HEADS)]
        gate = jnp.concatenate([gw] * HG_HEADS, axis=1) * _silu(proj_ref[rows, 3 * d:4 * d])
        og_ref[rows, :] = (o_all * jnp.concatenate(inv, axis=1) * gate).astype(BF16)

    n_chunks = tile // HG_CHUNK

    @pl.when(mergeable)
    def _():
        chunk_rows = [slice(c * HG_CHUNK, (c + 1) * HG_CHUNK) for c in range(n_chunks)]
        project(chunk_rows[0])
        for c in range(n_chunks):
            jobs = []
            if c + 1 < n_chunks:
                jobs = [functools.partial(project_part, chunk_rows[c + 1], part)
                        for part in range(4 * d // HG_PROJ_COLS)]
            if c > 0:
                jobs[0:0] = [functools.partial(finish, chunk_rows[c - 1])]
            recurrence(chunk_rows[c], merged=True, side_jobs=jobs)
        finish(chunk_rows[-1])

    @pl.when(jnp.logical_not(mergeable))
    def _():
        project(slice(0, tile))

        def body(c, carry):
            recurrence(pl.ds(pl.multiple_of(c * HG_CHUNK, HG_CHUNK), HG_CHUNK), merged=False)
            return carry

        lax.fori_loop(0, n_chunks, body, 0)
        finish(slice(0, tile))


def _resident():
    return pl.BlockSpec(memory_space=pltpu.VMEM)


def _hgrn_layer(x, w_in, lb_logits, gnorm_w, w_out, ln_g, ln_b):
    bsz, seq, d = x.shape
    tile = min(HG_TILE, seq)
    tri, lvl = _hgrn_constants()
    tok = pl.BlockSpec((1, tile, d), lambda b, j: (b, j, 0))
    return pl.pallas_call(
        _hgrn_kernel,
        grid=(bsz, seq // tile),
        in_specs=[tok] + [_resident()] * 8,
        out_specs=tok,
        out_shape=jax.ShapeDtypeStruct(x.shape, x.dtype),
        scratch_shapes=[
            pltpu.VMEM((tile, 4 * d), F32),
            pltpu.VMEM((tile, d), BF16),
            pltpu.VMEM((HG_HEADS, HG_DIM, HG_DIM), F32),
            pltpu.VMEM((HG_CHUNK, d), F32),
        ],
        compiler_params=pltpu.CompilerParams(
            dimension_semantics=("arbitrary", "arbitrary"), vmem_limit_bytes=VMEM_LIMIT),
        name="hgrn2_mixer",
    )(x, w_in.astype(BF16), lb_logits, gnorm_w.reshape(1, HG_DIM), w_out.astype(BF16),
      ln_g.reshape(1, d), ln_b.reshape(1, d), tri, lvl)


def _ffn_kernel(*refs, with_kv):
    if with_kv:
        (h_ref, win_ref, cw_ref, cb_ref, wout_ref, lng_ref, lnb_ref, wkv_ref,
         o_ref, k_ref, v_ref, u_ref, act_ref, carry_ref) = refs
    else:
        (h_ref, win_ref, cw_ref, cb_ref, wout_ref, lng_ref, lnb_ref,
         o_ref, u_ref, act_ref, carry_ref) = refs
    tile = h_ref.shape[1]
    f = wout_ref.shape[0]
    cols = FFN_COLS

    @pl.when(pl.program_id(1) == 0)
    def _():
        carry_ref[...] = jnp.zeros_like(carry_ref)

    h = h_ref[0]
    hb = h.astype(BF16)
    row8 = lax.broadcasted_iota(jnp.int32, (SUBLANES, cols), 0)

    def conv(cs):
        u = u_ref[:, cs]
        prev = carry_ref[:, cs]
        carry_ref[:, cs] = u[tile - SUBLANES:tile]
        p1 = prev[SUBLANES - 1:SUBLANES]
        p2 = prev[SUBLANES - 2:SUBLANES - 1]
        r1 = pltpu.roll(u, 1, 0)
        r2 = pltpu.roll(u, 2, 0)
        top1 = jnp.where(row8 == 0, p1, r1[0:SUBLANES])
        top2 = jnp.where(row8 == 0, p2, jnp.where(row8 == 1, p1, r2[0:SUBLANES]))
        um1 = jnp.concatenate([top1, r1[SUBLANES:]], axis=0)
        um2 = jnp.concatenate([top2, r2[SUBLANES:]], axis=0)
        return cw_ref[0:1, cs] * um2 + cw_ref[1:2, cs] * um1 + cw_ref[2:3, cs] * u + cb_ref[:, cs]

    for c in range(f // cols):
        ga = slice(c * cols, (c + 1) * cols)
        gb = slice(f + c * cols, f + (c + 1) * cols)
        u_ref[:, ga] = _dot(hb, win_ref[:, ga])
        u_ref[:, gb] = _dot(hb, win_ref[:, gb])
        act_ref[:, ga] = (_silu(conv(ga)) * conv(gb)).astype(BF16)

    blk = tile // FFN_OUT_BLOCKS
    for r in range(FFN_OUT_BLOCKS):
        rows = slice(r * blk, (r + 1) * blk)
        y = ALPHA * h_ref[0, rows, :] + _dot(act_ref[rows, :], wout_ref[...])
        out = _layer_norm(y, lng_ref[...], lnb_ref[...])
        o_ref[0, rows, :] = out
        if with_kv:
            kv = _dot(out.astype(BF16), wkv_ref[...])
            k_ref[0, rows, :] = kv[:, :KV_DIM].astype(BF16)
            v_ref[0, rows, :] = kv[:, KV_DIM:].astype(BF16)


def _ffn_layer(h, w_in, conv_w, conv_b, w_out, ln_g, ln_b, w_kv=None):
    bsz, seq, d = h.shape
    f = w_out.shape[0]
    tile = min(FFN_TILE, seq)
    with_kv = w_kv is not None
    tok = pl.BlockSpec((1, tile, d), lambda b, j: (b, j, 0))
    args = [h, w_in.astype(BF16), conv_w, conv_b.reshape(1, 2 * f), w_out.astype(BF16),
            ln_g.reshape(1, d), ln_b.reshape(1, d)]
    out_shape = [jax.ShapeDtypeStruct(h.shape, h.dtype)]
    out_specs = [tok]
    if with_kv:
        args.append(w_kv.astype(BF16))
        kv_spec = pl.BlockSpec((1, tile, KV_DIM), lambda b, j: (b, j, 0))
        out_shape += [jax.ShapeDtypeStruct((bsz, seq, KV_DIM), BF16)] * 2
        out_specs += [kv_spec, kv_spec]
    res = pl.pallas_call(
        functools.partial(_ffn_kernel, with_kv=with_kv),
        grid=(bsz, seq // tile),
        in_specs=[tok] + [_resident()] * (len(args) - 1),
        out_specs=out_specs,
        out_shape=out_shape,
        scratch_shapes=[
            pltpu.VMEM((tile, 2 * f), F32),
            pltpu.VMEM((tile, f), BF16),
            pltpu.VMEM((SUBLANES, 2 * f), F32),
        ],
        compiler_params=pltpu.CompilerParams(
            dimension_semantics=("arbitrary", "arbitrary"), vmem_limit_bytes=VMEM_LIMIT),
        name="conv_ffn_kv" if with_kv else "conv_ffn",
    )(*args)
    return res if with_kv else res[0]


def _t5_causal_bucket(dist):
    exact = REL_BUCKETS // 2
    dd = jnp.maximum(dist, 1).astype(F32)
    log_b = exact + (jnp.log(dd / exact) / np.log(REL_MAX_DIST / exact)
                     * (REL_BUCKETS - exact)).astype(jnp.int32)
    return jnp.where(dist < exact, dist, jnp.minimum(log_b, REL_BUCKETS - 1))


def _band_geometry():
    t = jnp.arange(SW_WINDOW)[:, None] + SW_WINDOW
    s = jnp.arange(2 * SW_WINDOW)[None, :]
    dist = t - s
    band = (dist >= 0) & (dist < SW_WINDOW)
    bucket = jnp.where(band, _t5_causal_bucket(jnp.maximum(dist, 0)), -1)
    first = jnp.where(s >= SW_WINDOW, bucket, -1)
    return jnp.stack([bucket, first]).astype(jnp.int32)


def _swa_kernel(h_ref, kc_ref, kp_ref, vc_ref, vp_ref, wq_ref, wo_ref, sink_ref, rel_ref, geo_ref,
                lng_ref, lnb_ref, o_ref, bias_ref, attn_ref, logit_ref, prob_ref, inv_ref):
    tile = h_ref.shape[1]
    w = SW_WINDOW
    j = pl.program_id(1)

    @pl.when((pl.program_id(0) == 0) & (j == 0))
    def _():
        for variant in range(2):
            def head_body(hq, carry, variant=variant):
                geo = geo_ref[variant]
                acc = jnp.full(geo.shape, -jnp.inf, F32)
                for bucket in range(REL_BUCKETS):
                    acc = jnp.where(geo == bucket, rel_ref[bucket, hq], acc)
                bias_ref[variant, hq] = acc
                return carry
            lax.fori_loop(0, SW_Q_HEADS, head_body, 0)

    h = h_ref[0]
    scale = SW_HEAD_DIM ** -0.5
    qb = (_dot(h.astype(BF16), wq_ref[...]) * scale).astype(BF16)
    kfull = jnp.concatenate([kp_ref[0], kc_ref[0]], axis=0)
    vfull = jnp.concatenate([vp_ref[0], vc_ref[0]], axis=0)
    lane = lax.broadcasted_iota(jnp.int32, (2 * w, LANES), 1)
    low = lane < SW_HEAD_DIM
    zero = jnp.zeros((2 * w, LANES), BF16)

    def both_halves(col, g):
        if g % 2 == 0:
            lo = jnp.where(low, col, zero)
            hi = pltpu.roll(lo, SW_HEAD_DIM, 1)
        else:
            hi = jnp.where(low, zero, col)
            lo = pltpu.roll(hi, SW_HEAD_DIM, 1)
        return jnp.concatenate([lo, hi], axis=0)

    pairs = SW_GROUP // 2
    for i in range(tile // w):
        kk = kfull[i * w:(i + 2) * w]
        vv = vfull[i * w:(i + 2) * w]
        variant = jnp.where(j == 0, 1, 0) if i == 0 else 0
        for g in range(SW_KV_HEADS):
            cs = slice((g // 2) * LANES, (g // 2 + 1) * LANES)
            kblk = both_halves(kk[:, cs], g)
            for pair in range(pairs):
                qc = g * pairs + pair
                logit_ref[i, qc] = _dot_nt(qb[i * w:(i + 1) * w, qc * LANES:(qc + 1) * LANES], kblk)
        for qc in range(SW_Q_HEADS // 2):
            for e_ in range(2):
                hq = 2 * qc + e_
                hs = slice(e_ * 2 * w, (e_ + 1) * 2 * w)
                lg = logit_ref[i, qc, :, hs] + bias_ref[variant, hq]
                sink = sink_ref[hq]
                m = jnp.maximum(jnp.max(lg, axis=-1, keepdims=True), sink)
                p = jnp.exp(lg - m)
                den = jnp.sum(p, axis=-1, keepdims=True) + jnp.exp(sink - m)
                prob_ref[i, qc, :, hs] = p.astype(BF16)
                inv_ref[i, qc, :, e_ * SW_HEAD_DIM:(e_ + 1) * SW_HEAD_DIM] = jnp.broadcast_to(
                    1.0 / den, (w, SW_HEAD_DIM))
        for g in range(SW_KV_HEADS):
            cs = slice((g // 2) * LANES, (g // 2 + 1) * LANES)
            vblk = both_halves(vv[:, cs], g)
            for pair in range(pairs):
                qc = g * pairs + pair
                o2 = _dot(prob_ref[i, qc], vblk) * inv_ref[i, qc]
                attn_ref[i * w:(i + 1) * w, qc * LANES:(qc + 1) * LANES] = o2.astype(BF16)

    mix = _dot(attn_ref[...], wo_ref[...])
    o_ref[0] = _layer_norm(ALPHA * h + mix, lng_ref[...], lnb_ref[...])


def _swa_layer(h, k, v, w_q, sinks, rel_bias, w_out, ln_g, ln_b):
    bsz, seq, d = h.shape
    tile = min(SWA_TILE, seq)
    w = SW_WINDOW
    bpt = tile // w
    tok = pl.BlockSpec((1, tile, d), lambda b, j: (b, j, 0))
    cur = pl.BlockSpec((1, tile, KV_DIM), lambda b, j: (b, j, 0))
    prev = pl.BlockSpec((1, w, KV_DIM), lambda b, j: (b, jnp.maximum(j * bpt - 1, 0), 0))
    smem = pl.BlockSpec(memory_space=pltpu.SMEM)
    return pl.pallas_call(
        _swa_kernel,
        grid=(bsz, seq // tile),
        in_specs=[tok, cur, prev, cur, prev, _resident(), _resident(), smem, smem, _resident(),
                  _resident(), _resident()],
        out_specs=tok,
        out_shape=jax.ShapeDtypeStruct(h.shape, h.dtype),
        scratch_shapes=[
            pltpu.VMEM((2, SW_Q_HEADS, w, 2 * w), F32),
            pltpu.VMEM((tile, d), BF16),
            pltpu.VMEM((bpt, SW_Q_HEADS // 2, w, 4 * w), F32),
            pltpu.VMEM((bpt, SW_Q_HEADS // 2, w, 4 * w), BF16),
            pltpu.VMEM((bpt, SW_Q_HEADS // 2, w, LANES), F32),
        ],
        compiler_params=pltpu.CompilerParams(
            dimension_semantics=("arbitrary", "arbitrary"), vmem_limit_bytes=VMEM_LIMIT),
        name="swa_sink_mixer",
    )(h, k, k, v, v, w_q.astype(BF16), w_out.astype(BF16), sinks, rel_bias, _band_geometry(),
      ln_g.reshape(1, d), ln_b.reshape(1, d))


def kernel(x, hgrn_w_in, hgrn_lb_logits, hgrn_gnorm_w, hgrn_w_out, swa_w_q, swa_sinks, swa_w_out,
           shared_w_kv, rel_bias, ffn_w_in, ffn_conv_w, ffn_conv_b, ffn_w_out,
           ln_mix_g, ln_mix_b, ln_ffn_g, ln_ffn_b):
    h = _hgrn_layer(x, hgrn_w_in[0], hgrn_lb_logits, hgrn_gnorm_w[0], hgrn_w_out[0],
                    ln_mix_g[0], ln_mix_b[0])
    h, k, v = _ffn_layer(h, ffn_w_in[0], ffn_conv_w[0], ffn_conv_b[0], ffn_w_out[0],
                         ln_ffn_g[0], ln_ffn_b[0], w_kv=shared_w_kv)
    h = _swa_layer(h, k, v, swa_w_q[0], swa_sinks[0], rel_bias, swa_w_out[0],
                   ln_mix_g[1], ln_mix_b[1])
    h = _ffn_layer(h, ffn_w_in[1], ffn_conv_w[1], ffn_conv_b[1], ffn_w_out[1],
                   ln_ffn_g[1], ln_ffn_b[1])
    return h
```

```python
import functools

import numpy as np
import jax
import jax.numpy as jnp
from jax import lax
from jax.experimental import pallas as pl
from jax.experimental.pallas import tpu as pltpu

D_MODEL = 1024
DEPTH = 2
HG_HEADS = 8
HG_DIM = D_MODEL // HG_HEADS
SW_Q_HEADS = 16
SW_KV_HEADS = 4
SW_HEAD_DIM = D_MODEL // SW_Q_HEADS
SW_GROUP = SW_Q_HEADS // SW_KV_HEADS
SW_WINDOW = 128
KV_DIM = SW_KV_HEADS * SW_HEAD_DIM
REL_BUCKETS = 32
REL_MAX_DIST = 128
FFN_DIM = 2816
ALPHA = (2.0 * DEPTH) ** 0.25
LN_EPS = 1e-5
RMS_EPS = 1e-6

LANES = 128
SUBLANES = 8
HG_CHUNK = 128
HG_TILE = 512
HG_PROJ_COLS = 256
FFN_TILE = 512
FFN_COLS = 256
FFN_OUT_BLOCKS = 1
SWA_TILE = 256
VMEM_LIMIT = 56 * 1024 * 1024

F32 = jnp.float32
BF16 = jnp.bfloat16


def _layer_norm(y, g, b):
    mu = jnp.mean(y, axis=-1, keepdims=True)
    yc = y - mu
    var = jnp.mean(yc * yc, axis=-1, keepdims=True)
    return yc * lax.rsqrt(var + LN_EPS) * g + b


def _silu(x):
    return x * jax.nn.sigmoid(x)


def _dot(a, b):
    return jnp.dot(a, b, preferred_element_type=F32)


def _dot_nt(a, b):
    return lax.dot_general(a, b, (((1,), (1,)), ((), ())), preferred_element_type=F32)


def _dot_tn(a, b):
    return lax.dot_general(a, b, (((0,), (0,)), ((), ())), preferred_element_type=F32)


HG_LEVEL_BLOCKS = (0, 1, 2, 4, 8, 16, 32, 64)
HG_MERGE = 16
HG_MAX_EXPONENT = 120.0
LOG2E = 1.4426950408889634


def _hgrn_constants():
    t = np.arange(HG_CHUNK)[:, None]
    s = np.arange(HG_CHUNK)[None, :]
    x = t ^ s
    lvl = np.where(x == 0, 0, np.floor(np.log2(np.maximum(x, 1))).astype(np.int64) + 1)
    lvl = np.where(s <= t, lvl, -1).astype(np.int32)
    tri = (s <= t).astype(np.float32)
    return jnp.asarray(tri, dtype=BF16), jnp.asarray(lvl)


def _level_reference(b, m):
    c, n = b.shape
    if m >= SUBLANES // 2:
        b3 = b.reshape(c // (2 * m), 2 * m, n)
        ref = jnp.broadcast_to(b3[:, m - 1:m, :], b3.shape)
        return ref.reshape(c, n)
    b3 = b.reshape(c // SUBLANES, SUBLANES, n)
    sub = lax.broadcasted_iota(jnp.int32, b3.shape, 1)
    ref = jnp.broadcast_to(b3[:, m - 1:m, :], b3.shape)
    for start in range(2 * m, SUBLANES, 2 * m):
        row = jnp.broadcast_to(b3[:, start + m - 1:start + m, :], b3.shape)
        ref = jnp.where(sub >= start, row, ref)
    return ref.reshape(c, n)


def _block_start(b, n_rows):
    c, n = b.shape
    b3 = b.reshape(c // n_rows, n_rows, n)
    return jnp.broadcast_to(b3[:, 0:1, :], b3.shape).reshape(c, n)


def _neg_abs(x):
    bits = lax.bitcast_convert_type(x, jnp.uint32) | jnp.uint32(0x80000000)
    return lax.bitcast_convert_type(bits, F32)


def _hgrn_kernel(x_ref, win_ref, lbl_ref, gw_ref, wout_ref, lng_ref, lnb_ref, tri_ref, lvl_ref,
                 o_ref, proj_ref, og_ref, state_ref, o_scr, sc_scr):
    tile = x_ref.shape[1]
    d = D_MODEL

    @pl.when(pl.program_id(1) == 0)
    def _():
        state_ref[...] = jnp.zeros_like(state_ref)

    lbl = lbl_ref[...]
    e = jnp.exp(lbl - jnp.max(lbl, axis=0, keepdims=True))
    lb = e[0:1] / jnp.sum(e, axis=0, keepdims=True)
    gw = gw_ref[...]
    tri = tri_ref[...]
    lvl = lvl_ref[...]
    mergeable = jnp.min(jnp.log2(lb)) * (HG_MERGE - 1) > -HG_MAX_EXPONENT

    def project_part(rows, part):
        cols = slice(part * HG_PROJ_COLS, (part + 1) * HG_PROJ_COLS)
        proj_ref[rows, cols] = _dot(x_ref[0, rows, :].astype(BF16), win_ref[:, cols])

    def project(rows):
        for part in range(4 * d // HG_PROJ_COLS):
            project_part(rows, part)

    def finish(rows):
        mix = _dot(og_ref[rows, :], wout_ref[...])
        o_ref[0, rows, :] = _layer_norm(ALPHA * x_ref[0, rows, :] + mix, lng_ref[...], lnb_ref[...])

    def recurrence(rows, merged, side_jobs=()):
        jobs = iter(side_jobs)

        def run_jobs(n):
            for _ in range(n):
                job = next(jobs, None)
                if job is not None:
                    job()

        run_jobs(3)
        fg = lb + (1.0 - lb) * jax.nn.sigmoid(proj_ref[rows, d:2 * d])
        logf = jnp.log(fg)
        k_all = 1.0 - fg
        q_all = _silu(proj_ref[rows, 0:d])
        hi = logf.astype(BF16)
        r1 = logf - hi.astype(F32)
        mid = r1.astype(BF16)
        lo = (r1 - mid.astype(F32)).astype(BF16)
        b_all = (_dot(jnp.concatenate([tri, tri], axis=1), jnp.concatenate([hi, mid], axis=0))
                 + _dot(tri, lo)) * LOG2E
        run_jobs(2)

        qb = q_all.astype(BF16)
        kb = k_all.astype(BF16)
        v_all = proj_ref[rows, 2 * d:3 * d]

        if merged:
            dlt = b_all - _block_start(b_all, HG_MERGE)
            pairs = [(None, qb * jnp.exp2(dlt).astype(BF16), kb * jnp.exp2(-dlt).astype(BF16))]
            first = HG_LEVEL_BLOCKS.index(HG_MERGE)
        else:
            pairs = [(0, qb, kb)]
            first = 1
        for level in range(first, len(HG_LEVEL_BLOCKS)):
            ex = jnp.exp2(_neg_abs(b_all - _level_reference(b_all, HG_LEVEL_BLOCKS[level]))).astype(BF16)
            pairs.append((level, qb * ex, kb * ex))
        b_last = b_all[HG_CHUNK - 1:HG_CHUNK, :]
        q_in = (q_all * jnp.exp2(b_all)).astype(BF16)
        k_dec = (k_all * jnp.exp2(b_last - b_all)).astype(BF16)
        decay = jnp.exp2(b_last)

        heads = [slice(h * HG_DIM, (h + 1) * HG_DIM) for h in range(HG_HEADS)]
        for h, col in enumerate(heads):
            scores = None
            for level, ql, kl in pairs:
                s_l = _dot_nt(ql[:, col], kl[:, col])
                if scores is None:
                    scores = jnp.where(lvl >= 0 if level is None else lvl == level, s_l, 0.0)
                else:
                    scores = jnp.where(lvl == level, s_l, scores)
            sc_scr[h] = scores.astype(BF16)
            run_jobs(1)
        for h, col in enumerate(heads):
            st = state_ref[h]
            vt = v_all[:, col].T.astype(BF16)
            o_scr[:, col] = _dot_nt(jnp.concatenate([q_in[:, col], sc_scr[h]], axis=1),
                                    jnp.concatenate([st.astype(BF16), vt], axis=1))
            state_ref[h] = st * decay[:, col] + _dot(vt, k_dec[:, col])
            run_jobs(h % 2)

        o_all = o_scr[...]
        sq = o_all * o_all
        inv = [jnp.broadcast_to(lax.rsqrt(jnp.mean(sq[:, col], axis=-1, keepdims=True) + RMS_EPS),
                                (HG_CHUNK, HG_DIM)) for col in heads]
        gate = jnp.concatenate([gw] * HG_HEADS, axis=1) * _silu(proj_ref[rows, 3 * d:4 * d])
        run_jobs(len(side_jobs))
        og_ref[rows, :] = (o_all * jnp.concatenate(inv, axis=1) * gate).astype(BF16)

    n_chunks = tile // HG_CHUNK

    @pl.when(mergeable)
    def _():
        chunk_rows = [slice(c * HG_CHUNK, (c + 1) * HG_CHUNK) for c in range(n_chunks)]
        project(chunk_rows[0])
        for c in range(n_chunks):
            jobs = []
            if c + 1 < n_chunks:
                jobs = [functools.partial(project_part, chunk_rows[c + 1], part)
                        for part in range(4 * d // HG_PROJ_COLS)]
            if c > 0:
                jobs[0:0] = [functools.partial(finish, chunk_rows[c - 1])]
            recurrence(chunk_rows[c], merged=True, side_jobs=jobs)
        finish(chunk_rows[-1])

    @pl.when(jnp.logical_not(mergeable))
    def _():
        project(slice(0, tile))

        def body(c, carry):
            recurrence(pl.ds(pl.multiple_of(c * HG_CHUNK, HG_CHUNK), HG_CHUNK), merged=False)
            return carry

        lax.fori_loop(0, n_chunks, body, 0)
        finish(slice(0, tile))


def _resident():
    return pl.BlockSpec(memory_space=pltpu.VMEM)


def _hgrn_layer(x, w_in, lb_logits, gnorm_w, w_out, ln_g, ln_b):
    bsz, seq, d = x.shape
    tile = min(HG_TILE, seq)
    tri, lvl = _hgrn_constants()
    tok = pl.BlockSpec((1, tile, d), lambda b, j: (b, j, 0))
    return pl.pallas_call(
        _hgrn_kernel,
        grid=(bsz, seq // tile),
        in_specs=[tok] + [_resident()] * 8,
        out_specs=tok,
        out_shape=jax.ShapeDtypeStruct(x.shape, x.dtype),
        scratch_shapes=[
            pltpu.VMEM((tile, 4 * d), F32),
            pltpu.VMEM((tile, d), BF16),
            pltpu.VMEM((HG_HEADS, HG_DIM, HG_DIM), F32),
            pltpu.VMEM((HG_CHUNK, d), F32),
            pltpu.VMEM((HG_HEADS, HG_CHUNK, HG_CHUNK), BF16),
        ],
        compiler_params=pltpu.CompilerParams(
            dimension_semantics=("arbitrary", "arbitrary"), vmem_limit_bytes=VMEM_LIMIT),
        name="hgrn2_mixer",
    )(x, w_in.astype(BF16), lb_logits, gnorm_w.reshape(1, HG_DIM), w_out.astype(BF16),
      ln_g.reshape(1, d), ln_b.reshape(1, d), tri, lvl)


def _ffn_kernel(*refs, with_kv):
    if with_kv:
        (h_ref, win_ref, cw_ref, cb_ref, wout_ref, lng_ref, lnb_ref, wkv_ref,
         o_ref, k_ref, v_ref, u_ref, act_ref, carry_ref) = refs
    else:
        (h_ref, win_ref, cw_ref, cb_ref, wout_ref, lng_ref, lnb_ref,
         o_ref, u_ref, act_ref, carry_ref) = refs
    tile = h_ref.shape[1]
    f = wout_ref.shape[0]
    cols = FFN_COLS

    @pl.when(pl.program_id(1) == 0)
    def _():
        carry_ref[...] = jnp.zeros_like(carry_ref)

    h = h_ref[0]
    hb = h.astype(BF16)
    row8 = lax.broadcasted_iota(jnp.int32, (SUBLANES, cols), 0)

    def conv(cs):
        u = u_ref[:, cs]
        prev = carry_ref[:, cs]
        carry_ref[:, cs] = u[tile - SUBLANES:tile]
        p1 = prev[SUBLANES - 1:SUBLANES]
        p2 = prev[SUBLANES - 2:SUBLANES - 1]
        r1 = pltpu.roll(u, 1, 0)
        r2 = pltpu.roll(u, 2, 0)
        top1 = jnp.where(row8 == 0, p1, r1[0:SUBLANES])
        top2 = jnp.where(row8 == 0, p2, jnp.where(row8 == 1, p1, r2[0:SUBLANES]))
        um1 = jnp.concatenate([top1, r1[SUBLANES:]], axis=0)
        um2 = jnp.concatenate([top2, r2[SUBLANES:]], axis=0)
        return cw_ref[0:1, cs] * um2 + cw_ref[1:2, cs] * um1 + cw_ref[2:3, cs] * u + cb_ref[:, cs]

    for c in range(f // cols):
        ga = slice(c * cols, (c + 1) * cols)
        gb = slice(f + c * cols, f + (c + 1) * cols)
        u_ref[:, ga] = _dot(hb, win_ref[:, ga])
        u_ref[:, gb] = _dot(hb, win_ref[:, gb])
        act_ref[:, ga] = (_silu(conv(ga)) * conv(gb)).astype(BF16)

    blk = tile // FFN_OUT_BLOCKS
    for r in range(FFN_OUT_BLOCKS):
        rows = slice(r * blk, (r + 1) * blk)
        y = ALPHA * h_ref[0, rows, :] + _dot(act_ref[rows, :], wout_ref[...])
        out = _layer_norm(y, lng_ref[...], lnb_ref[...])
        o_ref[0, rows, :] = out
        if with_kv:
            kv = _dot(out.astype(BF16), wkv_ref[...])
            k_ref[0, rows, :] = kv[:, :KV_DIM].astype(BF16)
            v_ref[0, rows, :] = kv[:, KV_DIM:].astype(BF16)


def _ffn_layer(h, w_in, conv_w, conv_b, w_out, ln_g, ln_b, w_kv=None):
    bsz, seq, d = h.shape
    f = w_out.shape[0]
    tile = min(FFN_TILE, seq)
    with_kv = w_kv is not None
    tok = pl.BlockSpec((1, tile, d), lambda b, j: (b, j, 0))
    args = [h, w_in.astype(BF16), conv_w, conv_b.reshape(1, 2 * f), w_out.astype(BF16),
            ln_g.reshape(1, d), ln_b.reshape(1, d)]
    out_shape = [jax.ShapeDtypeStruct(h.shape, h.dtype)]
    out_specs = [tok]
    if with_kv:
        args.append(w_kv.astype(BF16))
        kv_spec = pl.BlockSpec((1, tile, KV_DIM), lambda b, j: (b, j, 0))
        out_shape += [jax.ShapeDtypeStruct((bsz, seq, KV_DIM), BF16)] * 2
        out_specs += [kv_spec, kv_spec]
    res = pl.pallas_call(
        functools.partial(_ffn_kernel, with_kv=with_kv),
        grid=(bsz, seq // tile),
        in_specs=[tok] + [_resident()] * (len(args) - 1),
        out_specs=out_specs,
        out_shape=out_shape,
        scratch_shapes=[
            pltpu.VMEM((tile, 2 * f), F32),
            pltpu.VMEM((tile, f), BF16),
            pltpu.VMEM((SUBLANES, 2 * f), F32),
        ],
        compiler_params=pltpu.CompilerParams(
            dimension_semantics=("arbitrary", "arbitrary"), vmem_limit_bytes=VMEM_LIMIT),
        name="conv_ffn_kv" if with_kv else "conv_ffn",
    )(*args)
    return res if with_kv else res[0]


def _t5_causal_bucket(dist):
    exact = REL_BUCKETS // 2
    dd = jnp.maximum(dist, 1).astype(F32)
    log_b = exact + (jnp.log(dd / exact) / np.log(REL_MAX_DIST / exact)
                     * (REL_BUCKETS - exact)).astype(jnp.int32)
    return jnp.where(dist < exact, dist, jnp.minimum(log_b, REL_BUCKETS - 1))


def _band_geometry():
    t = jnp.arange(SW_WINDOW)[:, None] + SW_WINDOW
    s = jnp.arange(2 * SW_WINDOW)[None, :]
    dist = t - s
    band = (dist >= 0) & (dist < SW_WINDOW)
    bucket = jnp.where(band, _t5_causal_bucket(jnp.maximum(dist, 0)), -1)
    first = jnp.where(s >= SW_WINDOW, bucket, -1)
    return jnp.stack([bucket, first]).astype(jnp.int32)


def _swa_kernel(h_ref, kc_ref, kp_ref, vc_ref, vp_ref, wq_ref, wo_ref, sink_ref, rel_ref, geo_ref,
                lng_ref, lnb_ref, o_ref, bias_ref, attn_ref, logit_ref, prob_ref, inv_ref):
    tile = h_ref.shape[1]
    w = SW_WINDOW
    j = pl.program_id(1)

    @pl.when((pl.program_id(0) == 0) & (j == 0))
    def _():
        for variant in range(2):
            def head_body(hq, carry, variant=variant):
                geo = geo_ref[variant]
                acc = jnp.full(geo.shape, -jnp.inf, F32)
                for bucket in range(REL_BUCKETS):
                    acc = jnp.where(geo == bucket, rel_ref[bucket, hq], acc)
                bias_ref[variant, hq] = acc
                return carry
            lax.fori_loop(0, SW_Q_HEADS, head_body, 0)

    h = h_ref[0]
    scale = SW_HEAD_DIM ** -0.5
    qb = (_dot(h.astype(BF16), wq_ref[...]) * scale).astype(BF16)
    kfull = jnp.concatenate([kp_ref[0], kc_ref[0]], axis=0)
    vfull = jnp.concatenate([vp_ref[0], vc_ref[0]], axis=0)
    lane = lax.broadcasted_iota(jnp.int32, (2 * w, LANES), 1)
    low = lane < SW_HEAD_DIM
    zero = jnp.zeros((2 * w, LANES), BF16)

    def both_halves(col, g):
        if g % 2 == 0:
            lo = jnp.where(low, col, zero)
            hi = pltpu.roll(lo, SW_HEAD_DIM, 1)
        else:
            hi = jnp.where(low, zero, col)
            lo = pltpu.roll(hi, SW_HEAD_DIM, 1)
        return jnp.concatenate([lo, hi], axis=0)

    pairs = SW_GROUP // 2
    for i in range(tile // w):
        kk = kfull[i * w:(i + 2) * w]
        vv = vfull[i * w:(i + 2) * w]
        variant = jnp.where(j == 0, 1, 0) if i == 0 else 0
        for g in range(SW_KV_HEADS):
            cs = slice((g // 2) * LANES, (g // 2 + 1) * LANES)
            kblk = both_halves(kk[:, cs], g)
            for pair in range(pairs):
                qc = g * pairs + pair
                logit_ref[i, qc] = _dot_nt(qb[i * w:(i + 1) * w, qc * LANES:(qc + 1) * LANES], kblk)
        for qc in range(SW_Q_HEADS // 2):
            for e_ in range(2):
                hq = 2 * qc + e_
                hs = slice(e_ * 2 * w, (e_ + 1) * 2 * w)
                lg = logit_ref[i, qc, :, hs] + bias_ref[variant, hq]
                sink = sink_ref[hq]
                m = jnp.maximum(jnp.max(lg, axis=-1, keepdims=True), sink)
                p = jnp.exp(lg - m)
                den = jnp.sum(p, axis=-1, keepdims=True) + jnp.exp(sink - m)
                prob_ref[i, qc, :, hs] = p.astype(BF16)
                inv_ref[i, qc, :, e_ * SW_HEAD_DIM:(e_ + 1) * SW_HEAD_DIM] = jnp.broadcast_to(
                    1.0 / den, (w, SW_HEAD_DIM))
        for g in range(SW_KV_HEADS):
            cs = slice((g // 2) * LANES, (g // 2 + 1) * LANES)
            vblk = both_halves(vv[:, cs], g)
            for pair in range(pairs):
                qc = g * pairs + pair
                o2 = _dot(prob_ref[i, qc], vblk) * inv_ref[i, qc]
                attn_ref[i * w:(i + 1) * w, qc * LANES:(qc + 1) * LANES] = o2.astype(BF16)

    mix = _dot(attn_ref[...], wo_ref[...])
    o_ref[0] = _layer_norm(ALPHA * h + mix, lng_ref[...], lnb_ref[...])


def _swa_layer(h, k, v, w_q, sinks, rel_bias, w_out, ln_g, ln_b):
    bsz, seq, d = h.shape
    tile = min(SWA_TILE, seq)
    w = SW_WINDOW
    bpt = tile // w
    tok = pl.BlockSpec((1, tile, d), lambda b, j: (b, j, 0))
    cur = pl.BlockSpec((1, tile, KV_DIM), lambda b, j: (b, j, 0))
    prev = pl.BlockSpec((1, w, KV_DIM), lambda b, j: (b, jnp.maximum(j * bpt - 1, 0), 0))
    smem = pl.BlockSpec(memory_space=pltpu.SMEM)
    return pl.pallas_call(
        _swa_kernel,
        grid=(bsz, seq // tile),
        in_specs=[tok, cur, prev, cur, prev, _resident(), _resident(), smem, smem, _resident(),
                  _resident(), _resident()],
        out_specs=tok,
        out_shape=jax.ShapeDtypeStruct(h.shape, h.dtype),
        scratch_shapes=[
            pltpu.VMEM((2, SW_Q_HEADS, w, 2 * w), F32),
            pltpu.VMEM((tile, d), BF16),
            pltpu.VMEM((bpt, SW_Q_HEADS // 2, w, 4 * w), F32),
            pltpu.VMEM((bpt, SW_Q_HEADS // 2, w, 4 * w), BF16),
            pltpu.VMEM((bpt, SW_Q_HEADS // 2, w, LANES), F32),
        ],
        compiler_params=pltpu.CompilerParams(
            dimension_semantics=("arbitrary", "arbitrary"), vmem_limit_bytes=VMEM_LIMIT),
        name="swa_sink_mixer",
    )(h, k, k, v, v, w_q.astype(BF16), w_out.astype(BF16), sinks, rel_bias, _band_geometry(),
      ln_g.reshape(1, d), ln_b.reshape(1, d))


def kernel(x, hgrn_w_in, hgrn_lb_logits, hgrn_gnorm_w, hgrn_w_out, swa_w_q, swa_sinks, swa_w_out,
           shared_w_kv, rel_bias, ffn_w_in, ffn_conv_w, ffn_conv_b, ffn_w_out,
           ln_mix_g, ln_mix_b, ln_ffn_g, ln_ffn_b):
    h = _hgrn_layer(x, hgrn_w_in[0], hgrn_lb_logits, hgrn_gnorm_w[0], hgrn_w_out[0],
                    ln_mix_g[0], ln_mix_b[0])
    h, k, v = _ffn_layer(h, ffn_w_in[0], ffn_conv_w[0], ffn_conv_b[0], ffn_w_out[0],
                         ln_ffn_g[0], ln_ffn_b[0], w_kv=shared_w_kv)
    h = _swa_layer(h, k, v, swa_w_q[0], swa_sinks[0], rel_bias, swa_w_out[0],
                   ln_mix_g[1], ln_mix_b[1])
    h = _ffn_layer(h, ffn_w_in[1], ffn_conv_w[1], ffn_conv_b[1], ffn_w_out[1],
                   ln_ffn_g[1], ln_ffn_b[1])
    return h
```

```python
import functools

import numpy as np
import jax
import jax.numpy as jnp
from jax import lax
from jax.experimental import pallas as pl
from jax.experimental.pallas import tpu as pltpu

D_MODEL = 1024
DEPTH = 2
HG_HEADS = 8
HG_DIM = D_MODEL // HG_HEADS
SW_Q_HEADS = 16
SW_KV_HEADS = 4
SW_HEAD_DIM = D_MODEL // SW_Q_HEADS
SW_GROUP = SW_Q_HEADS // SW_KV_HEADS
SW_WINDOW = 128
KV_DIM = SW_KV_HEADS * SW_HEAD_DIM
REL_BUCKETS = 32
REL_MAX_DIST = 128
FFN_DIM = 2816
ALPHA = (2.0 * DEPTH) ** 0.25
LN_EPS = 1e-5
RMS_EPS = 1e-6

LANES = 128
SUBLANES = 8
HG_CHUNK = 128
HG_TILE = 512
HG_PROJ_COLS = 256
FFN_TILE = 512
FFN_COLS = 256
FFN_OUT_BLOCKS = 1
SWA_TILE = 512
VMEM_LIMIT = 56 * 1024 * 1024

F32 = jnp.float32
BF16 = jnp.bfloat16


def _layer_norm(y, g, b):
    mu = jnp.mean(y, axis=-1, keepdims=True)
    yc = y - mu
    var = jnp.mean(yc * yc, axis=-1, keepdims=True)
    return yc * lax.rsqrt(var + LN_EPS) * g + b


def _silu(x):
    return x * jax.nn.sigmoid(x)


def _dot(a, b):
    return jnp.dot(a, b, preferred_element_type=F32)


def _dot_nt(a, b):
    return lax.dot_general(a, b, (((1,), (1,)), ((), ())), preferred_element_type=F32)


def _dot_tn(a, b):
    return lax.dot_general(a, b, (((0,), (0,)), ((), ())), preferred_element_type=F32)


HG_LEVEL_BLOCKS = (0, 1, 2, 4, 8, 16, 32, 64)
HG_MERGE = 16
HG_MAX_EXPONENT = 120.0
LOG2E = 1.4426950408889634


def _hgrn_constants():
    t = np.arange(HG_CHUNK)[:, None]
    s = np.arange(HG_CHUNK)[None, :]
    x = t ^ s
    lvl = np.where(x == 0, 0, np.floor(np.log2(np.maximum(x, 1))).astype(np.int64) + 1)
    lvl = np.where(s <= t, lvl, -1).astype(np.int32)
    tri = (s <= t).astype(np.float32)
    return jnp.asarray(tri, dtype=BF16), jnp.asarray(lvl)


def _level_reference(b, m):
    c, n = b.shape
    if m >= SUBLANES // 2:
        b3 = b.reshape(c // (2 * m), 2 * m, n)
        ref = jnp.broadcast_to(b3[:, m - 1:m, :], b3.shape)
        return ref.reshape(c, n)
    b3 = b.reshape(c // SUBLANES, SUBLANES, n)
    sub = lax.broadcasted_iota(jnp.int32, b3.shape, 1)
    ref = jnp.broadcast_to(b3[:, m - 1:m, :], b3.shape)
    for start in range(2 * m, SUBLANES, 2 * m):
        row = jnp.broadcast_to(b3[:, start + m - 1:start + m, :], b3.shape)
        ref = jnp.where(sub >= start, row, ref)
    return ref.reshape(c, n)


def _block_start(b, n_rows):
    c, n = b.shape
    b3 = b.reshape(c // n_rows, n_rows, n)
    return jnp.broadcast_to(b3[:, 0:1, :], b3.shape).reshape(c, n)


def _neg_abs(x):
    bits = lax.bitcast_convert_type(x, jnp.uint32) | jnp.uint32(0x80000000)
    return lax.bitcast_convert_type(bits, F32)


def _hgrn_kernel(x_ref, win_ref, lbl_ref, gw_ref, wout_ref, lng_ref, lnb_ref, tri_ref, lvl_ref,
                 o_ref, proj_ref, og_ref, state_ref, o_scr, sc_scr):
    tile = x_ref.shape[1]
    d = D_MODEL

    @pl.when(pl.program_id(1) == 0)
    def _():
        state_ref[...] = jnp.zeros_like(state_ref)

    lbl = lbl_ref[...]
    e = jnp.exp(lbl - jnp.max(lbl, axis=0, keepdims=True))
    lb = e[0:1] / jnp.sum(e, axis=0, keepdims=True)
    gw = gw_ref[...]
    tri = tri_ref[...]
    lvl = lvl_ref[...]
    mergeable = jnp.min(jnp.log2(lb)) * (HG_MERGE - 1) > -HG_MAX_EXPONENT

    def project_part(rows, part):
        cols = slice(part * HG_PROJ_COLS, (part + 1) * HG_PROJ_COLS)
        proj_ref[rows, cols] = _dot(x_ref[0, rows, :].astype(BF16), win_ref[:, cols])

    def project(rows):
        for part in range(4 * d // HG_PROJ_COLS):
            project_part(rows, part)

    def finish(rows):
        mix = _dot(og_ref[rows, :], wout_ref[...])
        o_ref[0, rows, :] = _layer_norm(ALPHA * x_ref[0, rows, :] + mix, lng_ref[...], lnb_ref[...])

    def recurrence(rows, merged, side_jobs=()):
        jobs = iter(side_jobs)

        def run_jobs(n):
            for _ in range(n):
                job = next(jobs, None)
                if job is not None:
                    job()

        run_jobs(3)
        fg = lb + (1.0 - lb) * jax.nn.sigmoid(proj_ref[rows, d:2 * d])
        logf = jnp.log(fg)
        k_all = 1.0 - fg
        q_all = _silu(proj_ref[rows, 0:d])
        hi = logf.astype(BF16)
        r1 = logf - hi.astype(F32)
        mid = r1.astype(BF16)
        lo = (r1 - mid.astype(F32)).astype(BF16)
        b_all = (_dot(jnp.concatenate([tri, tri], axis=1), jnp.concatenate([hi, mid], axis=0))
                 + _dot(tri, lo)) * LOG2E
        run_jobs(2)

        qb = q_all.astype(BF16)
        kb = k_all.astype(BF16)
        v_all = proj_ref[rows, 2 * d:3 * d]

        if merged:
            dlt = b_all - _block_start(b_all, HG_MERGE)
            pairs = [(None, qb * jnp.exp2(dlt).astype(BF16), kb * jnp.exp2(-dlt).astype(BF16))]
            first = HG_LEVEL_BLOCKS.index(HG_MERGE)
        else:
            pairs = [(0, qb, kb)]
            first = 1
        for level in range(first, len(HG_LEVEL_BLOCKS)):
            ex = jnp.exp2(_neg_abs(b_all - _level_reference(b_all, HG_LEVEL_BLOCKS[level]))).astype(BF16)
            pairs.append((level, qb * ex, kb * ex))
        b_last = b_all[HG_CHUNK - 1:HG_CHUNK, :]
        q_in = (q_all * jnp.exp2(b_all)).astype(BF16)
        k_dec = (k_all * jnp.exp2(b_last - b_all)).astype(BF16)
        decay = jnp.exp2(b_last)

        heads = [slice(h * HG_DIM, (h + 1) * HG_DIM) for h in range(HG_HEADS)]
        for h, col in enumerate(heads):
            scores = None
            for level, ql, kl in pairs:
                s_l = _dot_nt(ql[:, col], kl[:, col])
                if scores is None:
                    scores = jnp.where(lvl >= 0 if level is None else lvl == level, s_l, 0.0)
                else:
                    scores = jnp.where(lvl == level, s_l, scores)
            sc_scr[h] = scores.astype(BF16)
            run_jobs(1)
        for h, col in enumerate(heads):
            st = state_ref[h]
            vt = v_all[:, col].T.astype(BF16)
            o_scr[:, col] = _dot_nt(jnp.concatenate([q_in[:, col], sc_scr[h]], axis=1),
                                    jnp.concatenate([st.astype(BF16), vt], axis=1))
            state_ref[h] = st * decay[:, col] + _dot(vt, k_dec[:, col])
            run_jobs(h % 2)

        o_all = o_scr[...]
        sq = o_all * o_all
        inv = [jnp.broadcast_to(lax.rsqrt(jnp.mean(sq[:, col], axis=-1, keepdims=True) + RMS_EPS),
                                (HG_CHUNK, HG_DIM)) for col in heads]
        gate = jnp.concatenate([gw] * HG_HEADS, axis=1) * _silu(proj_ref[rows, 3 * d:4 * d])
        run_jobs(len(side_jobs))
        og_ref[rows, :] = (o_all * jnp.concatenate(inv, axis=1) * gate).astype(BF16)

    n_chunks = tile // HG_CHUNK

    @pl.when(mergeable)
    def _():
        chunk_rows = [slice(c * HG_CHUNK, (c + 1) * HG_CHUNK) for c in range(n_chunks)]
        project(chunk_rows[0])
        for c in range(n_chunks):
            jobs = []
            if c + 1 < n_chunks:
                jobs = [functools.partial(project_part, chunk_rows[c + 1], part)
                        for part in range(4 * d // HG_PROJ_COLS)]
            if c > 0:
                jobs[0:0] = [functools.partial(finish, chunk_rows[c - 1])]
            recurrence(chunk_rows[c], merged=True, side_jobs=jobs)
        finish(chunk_rows[-1])

    @pl.when(jnp.logical_not(mergeable))
    def _():
        project(slice(0, tile))

        def body(c, carry):
            recurrence(pl.ds(pl.multiple_of(c * HG_CHUNK, HG_CHUNK), HG_CHUNK), merged=False)
            return carry

        lax.fori_loop(0, n_chunks, body, 0)
        finish(slice(0, tile))


def _resident():
    return pl.BlockSpec(memory_space=pltpu.VMEM)


def _hgrn_layer(x, w_in, lb_logits, gnorm_w, w_out, ln_g, ln_b):
    bsz, seq, d = x.shape
    tile = min(HG_TILE, seq)
    tri, lvl = _hgrn_constants()
    tok = pl.BlockSpec((1, tile, d), lambda b, j: (b, j, 0))
    return pl.pallas_call(
        _hgrn_kernel,
        grid=(bsz, seq // tile),
        in_specs=[tok] + [_resident()] * 8,
        out_specs=tok,
        out_shape=jax.ShapeDtypeStruct(x.shape, x.dtype),
        scratch_shapes=[
            pltpu.VMEM((tile, 4 * d), F32),
            pltpu.VMEM((tile, d), BF16),
            pltpu.VMEM((HG_HEADS, HG_DIM, HG_DIM), F32),
            pltpu.VMEM((HG_CHUNK, d), F32),
            pltpu.VMEM((HG_HEADS, HG_CHUNK, HG_CHUNK), BF16),
        ],
        compiler_params=pltpu.CompilerParams(
            dimension_semantics=("arbitrary", "arbitrary"), vmem_limit_bytes=VMEM_LIMIT),
        name="hgrn2_mixer",
    )(x, w_in.astype(BF16), lb_logits, gnorm_w.reshape(1, HG_DIM), w_out.astype(BF16),
      ln_g.reshape(1, d), ln_b.reshape(1, d), tri, lvl)


def _ffn_kernel(*refs, with_kv):
    if with_kv:
        (h_ref, win_ref, cw_ref, cb_ref, wout_ref, lng_ref, lnb_ref, wkv_ref,
         o_ref, k_ref, v_ref, u_ref, act_ref, carry_ref) = refs
    else:
        (h_ref, win_ref, cw_ref, cb_ref, wout_ref, lng_ref, lnb_ref,
         o_ref, u_ref, act_ref, carry_ref) = refs
    tile = h_ref.shape[1]
    f = wout_ref.shape[0]
    cols = FFN_COLS

    @pl.when(pl.program_id(1) == 0)
    def _():
        carry_ref[...] = jnp.zeros_like(carry_ref)

    h = h_ref[0]
    hb = h.astype(BF16)
    row8 = lax.broadcasted_iota(jnp.int32, (SUBLANES, cols), 0)

    def conv(cs):
        u = u_ref[:, cs]
        prev = carry_ref[:, cs]
        carry_ref[:, cs] = u[tile - SUBLANES:tile]
        p1 = prev[SUBLANES - 1:SUBLANES]
        p2 = prev[SUBLANES - 2:SUBLANES - 1]
        r1 = pltpu.roll(u, 1, 0)
        r2 = pltpu.roll(u, 2, 0)
        top1 = jnp.where(row8 == 0, p1, r1[0:SUBLANES])
        top2 = jnp.where(row8 == 0, p2, jnp.where(row8 == 1, p1, r2[0:SUBLANES]))
        um1 = jnp.concatenate([top1, r1[SUBLANES:]], axis=0)
        um2 = jnp.concatenate([top2, r2[SUBLANES:]], axis=0)
        return cw_ref[0:1, cs] * um2 + cw_ref[1:2, cs] * um1 + cw_ref[2:3, cs] * u + cb_ref[:, cs]

    for c in range(f // cols):
        ga = slice(c * cols, (c + 1) * cols)
        gb = slice(f + c * cols, f + (c + 1) * cols)
        u_ref[:, ga] = _dot(hb, win_ref[:, ga])
        u_ref[:, gb] = _dot(hb, win_ref[:, gb])
        act_ref[:, ga] = (_silu(conv(ga)) * conv(gb)).astype(BF16)

    blk = tile // FFN_OUT_BLOCKS
    for r in range(FFN_OUT_BLOCKS):
        rows = slice(r * blk, (r + 1) * blk)
        y = ALPHA * h_ref[0, rows, :] + _dot(act_ref[rows, :], wout_ref[...])
        out = _layer_norm(y, lng_ref[...], lnb_ref[...])
        o_ref[0, rows, :] = out
        if with_kv:
            kv = _dot(out.astype(BF16), wkv_ref[...])
            k_ref[0, rows, :] = kv[:, :KV_DIM].astype(BF16)
            v_ref[0, rows, :] = kv[:, KV_DIM:].astype(BF16)


def _ffn_layer(h, w_in, conv_w, conv_b, w_out, ln_g, ln_b, w_kv=None):
    bsz, seq, d = h.shape
    f = w_out.shape[0]
    tile = min(FFN_TILE, seq)
    with_kv = w_kv is not None
    tok = pl.BlockSpec((1, tile, d), lambda b, j: (b, j, 0))
    args = [h, w_in.astype(BF16), conv_w, conv_b.reshape(1, 2 * f), w_out.astype(BF16),
            ln_g.reshape(1, d), ln_b.reshape(1, d)]
    out_shape = [jax.ShapeDtypeStruct(h.shape, h.dtype)]
    out_specs = [tok]
    if with_kv:
        args.append(w_kv.astype(BF16))
        kv_spec = pl.BlockSpec((1, tile, KV_DIM), lambda b, j: (b, j, 0))
        out_shape += [jax.ShapeDtypeStruct((bsz, seq, KV_DIM), BF16)] * 2
        out_specs += [kv_spec, kv_spec]
    res = pl.pallas_call(
        functools.partial(_ffn_kernel, with_kv=with_kv),
        grid=(bsz, seq // tile),
        in_specs=[tok] + [_resident()] * (len(args) - 1),
        out_specs=out_specs,
        out_shape=out_shape,
        scratch_shapes=[
            pltpu.VMEM((tile, 2 * f), F32),
            pltpu.VMEM((tile, f), BF16),
            pltpu.VMEM((SUBLANES, 2 * f), F32),
        ],
        compiler_params=pltpu.CompilerParams(
            dimension_semantics=("arbitrary", "arbitrary"), vmem_limit_bytes=VMEM_LIMIT),
        name="conv_ffn_kv" if with_kv else "conv_ffn",
    )(*args)
    return res if with_kv else res[0]


def _t5_causal_bucket(dist):
    exact = REL_BUCKETS // 2
    dd = jnp.maximum(dist, 1).astype(F32)
    log_b = exact + (jnp.log(dd / exact) / np.log(REL_MAX_DIST / exact)
                     * (REL_BUCKETS - exact)).astype(jnp.int32)
    return jnp.where(dist < exact, dist, jnp.minimum(log_b, REL_BUCKETS - 1))


def _band_geometry():
    t = jnp.arange(SW_WINDOW)[:, None] + SW_WINDOW
    s = jnp.arange(2 * SW_WINDOW)[None, :]
    dist = t - s
    band = (dist >= 0) & (dist < SW_WINDOW)
    bucket = jnp.where(band, _t5_causal_bucket(jnp.maximum(dist, 0)), -1)
    first = jnp.where(s >= SW_WINDOW, bucket, -1)
    return jnp.stack([bucket, first]).astype(jnp.int32)


def _swa_kernel(h_ref, kc_ref, kp_ref, vc_ref, vp_ref, wq_ref, wo_ref, sink_ref, rel_ref, geo_ref,
                lng_ref, lnb_ref, o_ref, bias_ref, attn_ref, logit_ref, prob_ref, inv_ref):
    tile = h_ref.shape[1]
    w = SW_WINDOW
    j = pl.program_id(1)

    @pl.when((pl.program_id(0) == 0) & (j == 0))
    def _():
        for variant in range(2):
            def head_body(hq, carry, variant=variant):
                geo = geo_ref[variant]
                acc = jnp.full(geo.shape, -jnp.inf, F32)
                for bucket in range(REL_BUCKETS):
                    acc = jnp.where(geo == bucket, rel_ref[bucket, hq] * LOG2E, acc)
                bias_ref[variant, hq] = acc
                return carry
            lax.fori_loop(0, SW_Q_HEADS, head_body, 0)

    scale = SW_HEAD_DIM ** -0.5 * LOG2E
    kfull = jnp.concatenate([kp_ref[0], kc_ref[0]], axis=0)
    vfull = jnp.concatenate([vp_ref[0], vc_ref[0]], axis=0)
    lane = lax.broadcasted_iota(jnp.int32, (2 * w, LANES), 1)
    low = lane < SW_HEAD_DIM
    zero = jnp.zeros((2 * w, LANES), BF16)
    pairs = SW_GROUP // 2

    def both_halves(col, g):
        if g % 2 == 0:
            lo = jnp.where(low, col, zero)
            hi = pltpu.roll(lo, SW_HEAD_DIM, 1)
        else:
            hi = jnp.where(low, zero, col)
            lo = pltpu.roll(hi, SW_HEAD_DIM, 1)
        return jnp.concatenate([lo, hi], axis=0)

    def logits(i):
        rows = slice(i * w, (i + 1) * w)
        qb = (_dot(h_ref[0, rows, :].astype(BF16), wq_ref[...]) * scale).astype(BF16)
        kk = kfull[i * w:(i + 2) * w]
        for g in range(SW_KV_HEADS):
            kblk = both_halves(kk[:, (g // 2) * LANES:(g // 2 + 1) * LANES], g)
            for pair in range(pairs):
                qc = g * pairs + pair
                logit_ref[i, qc] = _dot_nt(qb[:, qc * LANES:(qc + 1) * LANES], kblk)

    def softmax(i):
        variant = jnp.where(j == 0, 1, 0) if i == 0 else 0
        for qc in range(SW_Q_HEADS // 2):
            for e_ in range(2):
                hq = 2 * qc + e_
                hs = slice(e_ * 2 * w, (e_ + 1) * 2 * w)
                lg = logit_ref[i, qc, :, hs] + bias_ref[variant, hq]
                sink = sink_ref[hq] * LOG2E
                m = jnp.maximum(jnp.max(lg, axis=-1, keepdims=True), sink)
                p = jnp.exp2(lg - m)
                den = jnp.sum(p, axis=-1, keepdims=True) + jnp.exp2(sink - m)
                prob_ref[i, qc, :, hs] = p.astype(BF16)
                inv_ref[i, qc, :, e_ * SW_HEAD_DIM:(e_ + 1) * SW_HEAD_DIM] = jnp.broadcast_to(
                    1.0 / den, (w, SW_HEAD_DIM))

    def values(i):
        rows = slice(i * w, (i + 1) * w)
        vv = vfull[i * w:(i + 2) * w]
        for g in range(SW_KV_HEADS):
            vblk = both_halves(vv[:, (g // 2) * LANES:(g // 2 + 1) * LANES], g)
            for pair in range(pairs):
                qc = g * pairs + pair
                o2 = _dot(prob_ref[i, qc], vblk) * inv_ref[i, qc]
                attn_ref[rows, qc * LANES:(qc + 1) * LANES] = o2.astype(BF16)

    def finish(i):
        rows = slice(i * w, (i + 1) * w)
        mix = _dot(attn_ref[rows, :], wo_ref[...])
        o_ref[0, rows, :] = _layer_norm(ALPHA * h_ref[0, rows, :] + mix, lng_ref[...], lnb_ref[...])

    n_blocks = tile // w
    logits(0)
    for i in range(n_blocks):
        if i + 1 < n_blocks:
            logits(i + 1)
        softmax(i)
        if i > 0:
            finish(i - 1)
        values(i)
    finish(n_blocks - 1)


def _swa_layer(h, k, v, w_q, sinks, rel_bias, w_out, ln_g, ln_b):
    bsz, seq, d = h.shape
    tile = min(SWA_TILE, seq)
    w = SW_WINDOW
    bpt = tile // w
    tok = pl.BlockSpec((1, tile, d), lambda b, j: (b, j, 0))
    cur = pl.BlockSpec((1, tile, KV_DIM), lambda b, j: (b, j, 0))
    prev = pl.BlockSpec((1, w, KV_DIM), lambda b, j: (b, jnp.maximum(j * bpt - 1, 0), 0))
    smem = pl.BlockSpec(memory_space=pltpu.SMEM)
    return pl.pallas_call(
        _swa_kernel,
        grid=(bsz, seq // tile),
        in_specs=[tok, cur, prev, cur, prev, _resident(), _resident(), smem, smem, _resident(),
                  _resident(), _resident()],
        out_specs=tok,
        out_shape=jax.ShapeDtypeStruct(h.shape, h.dtype),
        scratch_shapes=[
            pltpu.VMEM((2, SW_Q_HEADS, w, 2 * w), F32),
            pltpu.VMEM((tile, d), BF16),
            pltpu.VMEM((bpt, SW_Q_HEADS // 2, w, 4 * w), F32),
            pltpu.VMEM((bpt, SW_Q_HEADS // 2, w, 4 * w), BF16),
            pltpu.VMEM((bpt, SW_Q_HEADS // 2, w, LANES), F32),
        ],
        compiler_params=pltpu.CompilerParams(
            dimension_semantics=("arbitrary", "arbitrary"), vmem_limit_bytes=VMEM_LIMIT),
        name="swa_sink_mixer",
    )(h, k, k, v, v, w_q.astype(BF16), w_out.astype(BF16), sinks, rel_bias, _band_geometry(),
      ln_g.reshape(1, d), ln_b.reshape(1, d))


def kernel(x, hgrn_w_in, hgrn_lb_logits, hgrn_gnorm_w, hgrn_w_out, swa_w_q, swa_sinks, swa_w_out,
           shared_w_kv, rel_bias, ffn_w_in, ffn_conv_w, ffn_conv_b, ffn_w_out,
           ln_mix_g, ln_mix_b, ln_ffn_g, ln_ffn_b):
    h = _hgrn_layer(x, hgrn_w_in[0], hgrn_lb_logits, hgrn_gnorm_w[0], hgrn_w_out[0],
                    ln_mix_g[0], ln_mix_b[0])
    h, k, v = _ffn_layer(h, ffn_w_in[0], ffn_conv_w[0], ffn_conv_b[0], ffn_w_out[0],
                         ln_ffn_g[0], ln_ffn_b[0], w_kv=shared_w_kv)
    h = _swa_layer(h, k, v, swa_w_q[0], swa_sinks[0], rel_bias, swa_w_out[0],
                   ln_mix_g[1], ln_mix_b[1])
    h = _ffn_layer(h, ffn_w_in[1], ffn_conv_w[1], ffn_conv_b[1], ffn_w_out[1],
                   ln_ffn_g[1], ln_ffn_b[1])
    return h
```

```python
import functools

import numpy as np
import jax
import jax.numpy as jnp
from jax import lax
from jax.experimental import pallas as pl
from jax.experimental.pallas import tpu as pltpu

D_MODEL = 1024
DEPTH = 2
HG_HEADS = 8
HG_DIM = D_MODEL // HG_HEADS
SW_Q_HEADS = 16
SW_KV_HEADS = 4
SW_HEAD_DIM = D_MODEL // SW_Q_HEADS
SW_GROUP = SW_Q_HEADS // SW_KV_HEADS
SW_WINDOW = 128
KV_DIM = SW_KV_HEADS * SW_HEAD_DIM
REL_BUCKETS = 32
REL_MAX_DIST = 128
FFN_DIM = 2816
ALPHA = (2.0 * DEPTH) ** 0.25
LN_EPS = 1e-5
RMS_EPS = 1e-6

LANES = 128
SUBLANES = 8
HG_CHUNK = 128
HG_TILE = 512
HG_PROJ_COLS = 256
FFN_TILE = 512
FFN_COLS = 256
SWA_TILE = 512
VMEM_LIMIT = 56 * 1024 * 1024

F32 = jnp.float32
BF16 = jnp.bfloat16


def _layer_norm(y, g, b):
    mu = jnp.mean(y, axis=-1, keepdims=True)
    yc = y - mu
    var = jnp.mean(yc * yc, axis=-1, keepdims=True)
    return yc * lax.rsqrt(var + LN_EPS) * g + b


def _silu(x):
    return x * jax.nn.sigmoid(x)


def _dot(a, b):
    return jnp.dot(a, b, preferred_element_type=F32)


def _dot_nt(a, b):
    return lax.dot_general(a, b, (((1,), (1,)), ((), ())), preferred_element_type=F32)


def _dot_tn(a, b):
    return lax.dot_general(a, b, (((0,), (0,)), ((), ())), preferred_element_type=F32)


HG_LEVEL_BLOCKS = (0, 1, 2, 4, 8, 16, 32, 64)
HG_MERGE = 16
HG_MAX_EXPONENT = 120.0
LOG2E = 1.4426950408889634


def _hgrn_constants():
    t = np.arange(HG_CHUNK)[:, None]
    s = np.arange(HG_CHUNK)[None, :]
    x = t ^ s
    lvl = np.where(x == 0, 0, np.floor(np.log2(np.maximum(x, 1))).astype(np.int64) + 1)
    lvl = np.where(s <= t, lvl, -1).astype(np.int32)
    tri = (s <= t).astype(np.float32)
    return jnp.asarray(tri, dtype=BF16), jnp.asarray(lvl)


def _level_reference(b, m):
    c, n = b.shape
    if m >= SUBLANES // 2:
        b3 = b.reshape(c // (2 * m), 2 * m, n)
        ref = jnp.broadcast_to(b3[:, m - 1:m, :], b3.shape)
        return ref.reshape(c, n)
    b3 = b.reshape(c // SUBLANES, SUBLANES, n)
    sub = lax.broadcasted_iota(jnp.int32, b3.shape, 1)
    ref = jnp.broadcast_to(b3[:, m - 1:m, :], b3.shape)
    for start in range(2 * m, SUBLANES, 2 * m):
        row = jnp.broadcast_to(b3[:, start + m - 1:start + m, :], b3.shape)
        ref = jnp.where(sub >= start, row, ref)
    return ref.reshape(c, n)


def _block_start(b, n_rows):
    c, n = b.shape
    b3 = b.reshape(c // n_rows, n_rows, n)
    return jnp.broadcast_to(b3[:, 0:1, :], b3.shape).reshape(c, n)


def _neg_abs(x):
    bits = lax.bitcast_convert_type(x, jnp.uint32) | jnp.uint32(0x80000000)
    return lax.bitcast_convert_type(bits, F32)


def _hgrn_kernel(x_ref, win_ref, lbl_ref, gw_ref, wout_ref, lng_ref, lnb_ref, tri_ref, lvl_ref,
                 o_ref, proj_ref, og_ref, state_ref, o_scr, sc_scr):
    tile = x_ref.shape[1]
    d = D_MODEL

    @pl.when(pl.program_id(1) == 0)
    def _():
        state_ref[...] = jnp.zeros_like(state_ref)

    lbl = lbl_ref[...]
    e = jnp.exp(lbl - jnp.max(lbl, axis=0, keepdims=True))
    lb = e[0:1] / jnp.sum(e, axis=0, keepdims=True)
    gw = gw_ref[...]
    tri = tri_ref[...]
    lvl = lvl_ref[...]
    mergeable = jnp.min(jnp.log2(lb)) * (HG_MERGE - 1) > -HG_MAX_EXPONENT

    def project_part(rows, part):
        cols = slice(part * HG_PROJ_COLS, (part + 1) * HG_PROJ_COLS)
        proj_ref[rows, cols] = _dot(x_ref[0, rows, :].astype(BF16), win_ref[:, cols])

    def project(rows):
        for part in range(4 * d // HG_PROJ_COLS):
            project_part(rows, part)

    def finish(rows):
        mix = _dot(og_ref[rows, :], wout_ref[...])
        o_ref[0, rows, :] = _layer_norm(ALPHA * x_ref[0, rows, :] + mix, lng_ref[...], lnb_ref[...])

    def recurrence(rows, merged, side_jobs=()):
        n_slots = 2 + 2 * HG_HEADS
        slots = [[] for _ in range(n_slots)]
        for i, job in enumerate(side_jobs):
            slots[i * n_slots // len(side_jobs)].append(job)
        slot_iter = iter(slots)

        def run_jobs():
            for job in next(slot_iter):
                job()

        run_jobs()
        fg = lb + (1.0 - lb) * jax.nn.sigmoid(proj_ref[rows, d:2 * d])
        logf = jnp.log(fg)
        k_all = 1.0 - fg
        q_all = _silu(proj_ref[rows, 0:d])
        hi = logf.astype(BF16)
        r1 = logf - hi.astype(F32)
        mid = r1.astype(BF16)
        lo = (r1 - mid.astype(F32)).astype(BF16)
        b_all = (_dot(jnp.concatenate([tri, tri], axis=1), jnp.concatenate([hi, mid], axis=0))
                 + _dot(tri, lo)) * LOG2E
        run_jobs()

        qb = q_all.astype(BF16)
        kb = k_all.astype(BF16)
        v_all = proj_ref[rows, 2 * d:3 * d]

        if merged:
            dlt = b_all - _block_start(b_all, HG_MERGE)
            pairs = [(None, qb * jnp.exp2(dlt).astype(BF16), kb * jnp.exp2(-dlt).astype(BF16))]
            first = HG_LEVEL_BLOCKS.index(HG_MERGE)
        else:
            pairs = [(0, qb, kb)]
            first = 1
        for level in range(first, len(HG_LEVEL_BLOCKS)):
            ex = jnp.exp2(_neg_abs(b_all - _level_reference(b_all, HG_LEVEL_BLOCKS[level]))).astype(BF16)
            pairs.append((level, qb * ex, kb * ex))
        b_last = b_all[HG_CHUNK - 1:HG_CHUNK, :]
        q_in = (q_all * jnp.exp2(b_all)).astype(BF16)
        k_dec = (k_all * jnp.exp2(b_last - b_all)).astype(BF16)
        decay = jnp.exp2(b_last)

        heads = [slice(h * HG_DIM, (h + 1) * HG_DIM) for h in range(HG_HEADS)]
        for h, col in enumerate(heads):
            scores = None
            for level, ql, kl in pairs:
                s_l = _dot_nt(ql[:, col], kl[:, col])
                if scores is None:
                    scores = jnp.where(lvl >= 0 if level is None else lvl == level, s_l, 0.0)
                else:
                    scores = jnp.where(lvl == level, s_l, scores)
            sc_scr[h] = scores.astype(BF16)
            run_jobs()
        for h, col in enumerate(heads):
            st = state_ref[h]
            vt = v_all[:, col].T.astype(BF16)
            o_scr[:, col] = _dot_nt(jnp.concatenate([q_in[:, col], sc_scr[h]], axis=1),
                                    jnp.concatenate([st.astype(BF16), vt], axis=1))
            state_ref[h] = st * decay[:, col] + _dot(vt, k_dec[:, col])
            run_jobs()

        o_all = o_scr[...]
        sq = o_all * o_all
        inv = [jnp.broadcast_to(lax.rsqrt(jnp.mean(sq[:, col], axis=-1, keepdims=True) + RMS_EPS),
                                (HG_CHUNK, HG_DIM)) for col in heads]
        gate = jnp.concatenate([gw] * HG_HEADS, axis=1) * _silu(proj_ref[rows, 3 * d:4 * d])
        og_ref[rows, :] = (o_all * jnp.concatenate(inv, axis=1) * gate).astype(BF16)

    n_chunks = tile // HG_CHUNK

    @pl.when(mergeable)
    def _():
        chunk_rows = [slice(c * HG_CHUNK, (c + 1) * HG_CHUNK) for c in range(n_chunks)]
        project(chunk_rows[0])
        for c in range(n_chunks):
            jobs = []
            if c + 1 < n_chunks:
                jobs = [functools.partial(project_part, chunk_rows[c + 1], part)
                        for part in range(4 * d // HG_PROJ_COLS)]
            if c > 0:
                jobs[0:0] = [functools.partial(finish, chunk_rows[c - 1])]
            recurrence(chunk_rows[c], merged=True, side_jobs=jobs)
        finish(chunk_rows[-1])

    @pl.when(jnp.logical_not(mergeable))
    def _():
        project(slice(0, tile))

        def body(c, carry):
            recurrence(pl.ds(pl.multiple_of(c * HG_CHUNK, HG_CHUNK), HG_CHUNK), merged=False)
            return carry

        lax.fori_loop(0, n_chunks, body, 0)
        finish(slice(0, tile))


def _resident():
    return pl.BlockSpec(memory_space=pltpu.VMEM)


def _hgrn_layer(x, w_in, lb_logits, gnorm_w, w_out, ln_g, ln_b):
    bsz, seq, d = x.shape
    tile = min(HG_TILE, seq)
    tri, lvl = _hgrn_constants()
    tok = pl.BlockSpec((1, tile, d), lambda b, j: (b, j, 0))
    return pl.pallas_call(
        _hgrn_kernel,
        grid=(bsz, seq // tile),
        in_specs=[tok] + [_resident()] * 8,
        out_specs=tok,
        out_shape=jax.ShapeDtypeStruct(x.shape, x.dtype),
        scratch_shapes=[
            pltpu.VMEM((tile, 4 * d), F32),
            pltpu.VMEM((tile, d), BF16),
            pltpu.VMEM((HG_HEADS, HG_DIM, HG_DIM), F32),
            pltpu.VMEM((HG_CHUNK, d), F32),
            pltpu.VMEM((HG_HEADS, HG_CHUNK, HG_CHUNK), BF16),
        ],
        compiler_params=pltpu.CompilerParams(
            dimension_semantics=("arbitrary", "arbitrary"), vmem_limit_bytes=VMEM_LIMIT),
        name="hgrn2_mixer",
    )(x, w_in.astype(BF16), lb_logits, gnorm_w.reshape(1, HG_DIM), w_out.astype(BF16),
      ln_g.reshape(1, d), ln_b.reshape(1, d), tri, lvl)


def _ffn_kernel(*refs, with_kv):
    if with_kv:
        (h_ref, win_ref, cw_ref, cb_ref, wout_ref, lng_ref, lnb_ref, wkv_ref,
         o_ref, k_ref, v_ref, u_ref, act_ref, carry_ref) = refs
    else:
        (h_ref, win_ref, cw_ref, cb_ref, wout_ref, lng_ref, lnb_ref,
         o_ref, u_ref, act_ref, carry_ref) = refs
    tile = h_ref.shape[1]
    f = wout_ref.shape[0]
    cols = FFN_COLS

    @pl.when(pl.program_id(1) == 0)
    def _():
        carry_ref[...] = jnp.zeros_like(carry_ref)

    h = h_ref[0]
    hb = h.astype(BF16)
    row8 = lax.broadcasted_iota(jnp.int32, (SUBLANES, cols), 0)

    def conv(cs):
        u = u_ref[:, cs]
        prev = carry_ref[:, cs]
        carry_ref[:, cs] = u[tile - SUBLANES:tile]
        p1 = prev[SUBLANES - 1:SUBLANES]
        p2 = prev[SUBLANES - 2:SUBLANES - 1]
        r1 = pltpu.roll(u, 1, 0)
        r2 = pltpu.roll(u, 2, 0)
        top1 = jnp.where(row8 == 0, p1, r1[0:SUBLANES])
        top2 = jnp.where(row8 == 0, p2, jnp.where(row8 == 1, p1, r2[0:SUBLANES]))
        um1 = jnp.concatenate([top1, r1[SUBLANES:]], axis=0)
        um2 = jnp.concatenate([top2, r2[SUBLANES:]], axis=0)
        return cw_ref[0:1, cs] * um2 + cw_ref[1:2, cs] * um1 + cw_ref[2:3, cs] * u + cb_ref[:, cs]

    for c in range(f // cols):
        ga = slice(c * cols, (c + 1) * cols)
        gb = slice(f + c * cols, f + (c + 1) * cols)
        u_ref[:, ga] = _dot(hb, win_ref[:, ga])
        u_ref[:, gb] = _dot(hb, win_ref[:, gb])
        act_ref[:, ga] = (_silu(conv(ga)) * conv(gb)).astype(BF16)

    out = _layer_norm(ALPHA * h + _dot(act_ref[...], wout_ref[...]), lng_ref[...], lnb_ref[...])
    o_ref[0] = out
    if with_kv:
        kv = _dot(out.astype(BF16), wkv_ref[...])
        k_ref[0] = kv[:, :KV_DIM].astype(BF16)
        v_ref[0] = kv[:, KV_DIM:].astype(BF16)


def _ffn_layer(h, w_in, conv_w, conv_b, w_out, ln_g, ln_b, w_kv=None):
    bsz, seq, d = h.shape
    f = w_out.shape[0]
    tile = min(FFN_TILE, seq)
    with_kv = w_kv is not None
    tok = pl.BlockSpec((1, tile, d), lambda b, j: (b, j, 0))
    args = [h, w_in.astype(BF16), conv_w, conv_b.reshape(1, 2 * f), w_out.astype(BF16),
            ln_g.reshape(1, d), ln_b.reshape(1, d)]
    out_shape = [jax.ShapeDtypeStruct(h.shape, h.dtype)]
    out_specs = [tok]
    if with_kv:
        args.append(w_kv.astype(BF16))
        kv_spec = pl.BlockSpec((1, tile, KV_DIM), lambda b, j: (b, j, 0))
        out_shape += [jax.ShapeDtypeStruct((bsz, seq, KV_DIM), BF16)] * 2
        out_specs += [kv_spec, kv_spec]
    res = pl.pallas_call(
        functools.partial(_ffn_kernel, with_kv=with_kv),
        grid=(bsz, seq // tile),
        in_specs=[tok] + [_resident()] * (len(args) - 1),
        out_specs=out_specs,
        out_shape=out_shape,
        scratch_shapes=[
            pltpu.VMEM((tile, 2 * f), F32),
            pltpu.VMEM((tile, f), BF16),
            pltpu.VMEM((SUBLANES, 2 * f), F32),
        ],
        compiler_params=pltpu.CompilerParams(
            dimension_semantics=("arbitrary", "arbitrary"), vmem_limit_bytes=VMEM_LIMIT),
        name="conv_ffn_kv" if with_kv else "conv_ffn",
    )(*args)
    return res if with_kv else res[0]


def _t5_causal_bucket(dist):
    exact = REL_BUCKETS // 2
    dd = jnp.maximum(dist, 1).astype(F32)
    log_b = exact + (jnp.log(dd / exact) / np.log(REL_MAX_DIST / exact)
                     * (REL_BUCKETS - exact)).astype(jnp.int32)
    return jnp.where(dist < exact, dist, jnp.minimum(log_b, REL_BUCKETS - 1))


def _band_geometry():
    t = jnp.arange(SW_WINDOW)[:, None] + SW_WINDOW
    s = jnp.arange(2 * SW_WINDOW)[None, :]
    dist = t - s
    band = (dist >= 0) & (dist < SW_WINDOW)
    bucket = jnp.where(band, _t5_causal_bucket(jnp.maximum(dist, 0)), -1)
    first = jnp.where(s >= SW_WINDOW, bucket, -1)
    return jnp.stack([bucket, first]).astype(jnp.int32)


def _swa_kernel(h_ref, kc_ref, kp_ref, vc_ref, vp_ref, wq_ref, wo_ref, sink_ref, rel_ref, geo_ref,
                lng_ref, lnb_ref, o_ref, bias_ref, attn_ref, logit_ref, prob_ref, inv_ref):
    tile = h_ref.shape[1]
    w = SW_WINDOW
    j = pl.program_id(1)

    @pl.when((pl.program_id(0) == 0) & (j == 0))
    def _():
        for variant in range(2):
            def head_body(hq, carry, variant=variant):
                geo = geo_ref[variant]
                acc = jnp.full(geo.shape, -jnp.inf, F32)
                for bucket in range(REL_BUCKETS):
                    acc = jnp.where(geo == bucket, rel_ref[bucket, hq] * LOG2E, acc)
                bias_ref[variant, hq] = acc
                return carry
            lax.fori_loop(0, SW_Q_HEADS, head_body, 0)

    scale = SW_HEAD_DIM ** -0.5 * LOG2E
    kfull = jnp.concatenate([kp_ref[0], kc_ref[0]], axis=0)
    vfull = jnp.concatenate([vp_ref[0], vc_ref[0]], axis=0)
    lane = lax.broadcasted_iota(jnp.int32, (2 * w, LANES), 1)
    low = lane < SW_HEAD_DIM
    zero = jnp.zeros((2 * w, LANES), BF16)
    pairs = SW_GROUP // 2

    def both_halves(col, g):
        if g % 2 == 0:
            lo = jnp.where(low, col, zero)
            hi = pltpu.roll(lo, SW_HEAD_DIM, 1)
        else:
            hi = jnp.where(low, zero, col)
            lo = pltpu.roll(hi, SW_HEAD_DIM, 1)
        return jnp.concatenate([lo, hi], axis=0)

    def logits(i):
        rows = slice(i * w, (i + 1) * w)
        q = _dot(h_ref[0, rows, :].astype(BF16), wq_ref[...]) * scale
        kk = kfull[i * w:(i + 2) * w]
        for g in range(SW_KV_HEADS):
            kblk = both_halves(kk[:, (g // 2) * LANES:(g // 2 + 1) * LANES], g)
            qs = jnp.concatenate([q[:, (g * pairs + pair) * LANES:(g * pairs + pair + 1) * LANES]
                                  for pair in range(pairs)], axis=0).astype(BF16)
            logit_ref[i, g * pairs:(g + 1) * pairs] = _dot_nt(qs, kblk).reshape(pairs, w, 4 * w)

    def softmax(i):
        variant = jnp.where(j == 0, 1, 0) if i == 0 else 0
        for qc in range(SW_Q_HEADS // 2):
            for e_ in range(2):
                hq = 2 * qc + e_
                hs = slice(e_ * 2 * w, (e_ + 1) * 2 * w)
                lg = logit_ref[i, qc, :, hs] + bias_ref[variant, hq]
                sink = sink_ref[hq] * LOG2E
                m = jnp.maximum(jnp.max(lg, axis=-1, keepdims=True), sink)
                p = jnp.exp2(lg - m)
                den = jnp.sum(p, axis=-1, keepdims=True) + jnp.exp2(sink - m)
                prob_ref[i, qc, :, hs] = p.astype(BF16)
                inv_ref[i, qc, :, e_ * SW_HEAD_DIM:(e_ + 1) * SW_HEAD_DIM] = jnp.broadcast_to(
                    1.0 / den, (w, SW_HEAD_DIM))

    def values(i):
        rows = slice(i * w, (i + 1) * w)
        vv = vfull[i * w:(i + 2) * w]
        for g in range(SW_KV_HEADS):
            vblk = both_halves(vv[:, (g // 2) * LANES:(g // 2 + 1) * LANES], g)
            group = slice(g * pairs, (g + 1) * pairs)
            o2 = _dot(prob_ref[i, group].reshape(pairs * w, 4 * w), vblk) * inv_ref[i, group].reshape(pairs * w, LANES)
            attn_ref[rows, g * pairs * LANES:(g + 1) * pairs * LANES] = jnp.concatenate(
                [o2[pair * w:(pair + 1) * w] for pair in range(pairs)], axis=1).astype(BF16)

    def finish(i):
        rows = slice(i * w, (i + 1) * w)
        mix = _dot(attn_ref[rows, :], wo_ref[...])
        o_ref[0, rows, :] = _layer_norm(ALPHA * h_ref[0, rows, :] + mix, lng_ref[...], lnb_ref[...])

    n_blocks = tile // w
    logits(0)
    for i in range(n_blocks):
        if i + 1 < n_blocks:
            logits(i + 1)
        softmax(i)
        if i > 0:
            finish(i - 1)
        values(i)
    finish(n_blocks - 1)


def _swa_layer(h, k, v, w_q, sinks, rel_bias, w_out, ln_g, ln_b):
    bsz, seq, d = h.shape
    tile = min(SWA_TILE, seq)
    w = SW_WINDOW
    bpt = tile // w
    tok = pl.BlockSpec((1, tile, d), lambda b, j: (b, j, 0))
    cur = pl.BlockSpec((1, tile, KV_DIM), lambda b, j: (b, j, 0))
    prev = pl.BlockSpec((1, w, KV_DIM), lambda b, j: (b, jnp.maximum(j * bpt - 1, 0), 0))
    smem = pl.BlockSpec(memory_space=pltpu.SMEM)
    return pl.pallas_call(
        _swa_kernel,
        grid=(bsz, seq // tile),
        in_specs=[tok, cur, prev, cur, prev, _resident(), _resident(), smem, smem, _resident(),
                  _resident(), _resident()],
        out_specs=tok,
        out_shape=jax.ShapeDtypeStruct(h.shape, h.dtype),
        scratch_shapes=[
            pltpu.VMEM((2, SW_Q_HEADS, w, 2 * w), F32),
            pltpu.VMEM((tile, d), BF16),
            pltpu.VMEM((bpt, SW_Q_HEADS // 2, w, 4 * w), F32),
            pltpu.VMEM((bpt, SW_Q_HEADS // 2, w, 4 * w), BF16),
            pltpu.VMEM((bpt, SW_Q_HEADS // 2, w, LANES), F32),
        ],
        compiler_params=pltpu.CompilerParams(
            dimension_semantics=("arbitrary", "arbitrary"), vmem_limit_bytes=VMEM_LIMIT),
        name="swa_sink_mixer",
    )(h, k, k, v, v, w_q.astype(BF16), w_out.astype(BF16), sinks, rel_bias, _band_geometry(),
      ln_g.reshape(1, d), ln_b.reshape(1, d))


def kernel(x, hgrn_w_in, hgrn_lb_logits, hgrn_gnorm_w, hgrn_w_out, swa_w_q, swa_sinks, swa_w_out,
           shared_w_kv, rel_bias, ffn_w_in, ffn_conv_w, ffn_conv_b, ffn_w_out,
           ln_mix_g, ln_mix_b, ln_ffn_g, ln_ffn_b):
    h = _hgrn_layer(x, hgrn_w_in[0], hgrn_lb_logits, hgrn_gnorm_w[0], hgrn_w_out[0],
                    ln_mix_g[0], ln_mix_b[0])
    h, k, v = _ffn_layer(h, ffn_w_in[0], ffn_conv_w[0], ffn_conv_b[0], ffn_w_out[0],
                         ln_ffn_g[0], ln_ffn_b[0], w_kv=shared_w_kv)
    h = _swa_layer(h, k, v, swa_w_q[0], swa_sinks[0], rel_bias, swa_w_out[0],
                   ln_mix_g[1], ln_mix_b[1])
    h = _ffn_layer(h, ffn_w_in[1], ffn_conv_w[1], ffn_conv_b[1], ffn_w_out[1],
                   ln_ffn_g[1], ln_ffn_b[1])
    return h
```
